```python
import math
import jax
import jax.numpy as jnp
from jax import lax
import numpy as np


D_MODEL = 1024
BATCH = 8
SEQ = 4096
DEPTH = 2

CTX_LEN = 256
GRID_W = 64
NA_HEADS = 8
NA_HEAD_DIM = 64
NA_WIDTH = NA_HEADS * NA_HEAD_DIM
NA_WIN_ROWS = 8
NA_WIN_COLS = 16
SSM_HEADS = 16
SSM_HEAD_DIM = 64
SSM_INNER = SSM_HEADS * SSM_HEAD_DIM
SSM_GROUPS = 4
SSM_STATE = 128
SSM_CONV = 3
SSD_CHUNK = 128
SSM_CONV_DIM = SSM_INNER + 2 * SSM_GROUPS * SSM_STATE
DT_MIN = 0.001
DT_MAX = 0.1
POOL_WINDOWS = (2, 4, 8, 16)
N_POOL = len(POOL_WINDOWS)
POOL_GROUP = D_MODEL // N_POOL
FFN_HIDDEN = 2816
FFN_CONV = 3
RMS_EPS = 1e-6
N_EVEN = (DEPTH + 1) // 2
N_ODD = DEPTH // 2
IN_SPLITS = (NA_WIDTH, SSM_INNER, NA_WIDTH, NA_WIDTH, SSM_CONV_DIM, 2 * SSM_HEADS)
IN_WIDTH = sum(IN_SPLITS)
CTX_KV_OFFSET = NA_WIDTH + SSM_INNER
MIX_WIDTH = NA_WIDTH + SSM_INNER

kernel_name = "hybrid_natten_ssd_pool_dit_block"


def rmsnorm(x, g):
    xf = x.astype(jnp.float32)
    xf = xf * lax.rsqrt(jnp.mean(xf * xf, axis=-1, keepdims=True) + RMS_EPS)
    return xf.astype(x.dtype) * g


def modulate(h, shift, scale):
    return h * (1 + scale) + shift


def split_cols(p, sizes):
    return jnp.split(p, [int(v) for v in np.cumsum(sizes)[:-1]], axis=-1)


def dwconv_centred(x, w, b):
    k = w.shape[0]
    pad = k // 2
    l = x.shape[1]
    xp = jnp.pad(x, ((0, 0), (pad, pad), (0, 0)))
    return sum(xp[:, j:j + l] * w[j] for j in range(k)) + b


def neighbourhood_attention(q, k, v, kc, vc, rpb):
    bsz, s, h, dh = q.shape
    rows = s // GRID_W
    kh = min(NA_WIN_ROWS, rows)
    kw = NA_WIN_COLS
    scale = dh ** -0.5
    qg = q.reshape(bsz, rows, GRID_W, h, dh)
    kg = k.reshape(bsz, rows, GRID_W, h, dh)
    vg = v.reshape(bsz, rows, GRID_W, h, dh)
    cols = jnp.arange(GRID_W)
    col_idx = jnp.clip(cols - kw // 2, 0, GRID_W - kw)[:, None] + jnp.arange(kw)
    rel_col = col_idx - cols[:, None] + (NA_WIN_COLS - 1)

    def row_block(r):
        rs = jnp.clip(r - kh // 2, 0, rows - kh)
        q_r = lax.dynamic_index_in_dim(qg, r, axis=1, keepdims=False)
        k_win = lax.dynamic_slice_in_dim(kg, rs, kh, axis=1)[:, :, col_idx]
        v_win = lax.dynamic_slice_in_dim(vg, rs, kh, axis=1)[:, :, col_idx]
        rel_row = rs + jnp.arange(kh) - r + (NA_WIN_ROWS - 1)
        bias = rpb[:, rel_row[None, :, None], rel_col[:, None, :]]
        s_lat = jnp.einsum('bqhd,biqjhd->bhqij', q_r, k_win).astype(jnp.float32) * scale + bias.astype(jnp.float32)
        s_ctx = jnp.einsum('bqhd,blhd->bhql', q_r, kc).astype(jnp.float32) * scale
        p = jax.nn.softmax(jnp.concatenate([s_lat.reshape(bsz, h, GRID_W, kh * kw), s_ctx], axis=-1), axis=-1).astype(v.dtype)
        p_lat = p[..., :kh * kw].reshape(bsz, h, GRID_W, kh, kw)
        p_ctx = p[..., kh * kw:]
        return jnp.einsum('bhqij,biqjhd->bqhd', p_lat, v_win) + jnp.einsum('bhql,blhd->bqhd', p_ctx, vc)

    out = lax.map(row_block, jnp.arange(rows))
    return out.transpose(1, 0, 2, 3, 4).reshape(bsz, s, h * dh)


def context_attention(qc, kc, vc):
    bsz, l, h, dh = qc.shape
    s = jnp.einsum('blhd,bmhd->bhlm', qc, kc).astype(jnp.float32) * dh ** -0.5
    p = jax.nn.softmax(s, axis=-1).astype(vc.dtype)
    return jnp.einsum('bhlm,bmhd->blhd', p, vc).reshape(bsz, l, h * dh)


def ssd_scan(x, a, bm, cm, h0):
    bsz, l, h, p = x.shape
    g, n = bm.shape[-2:]
    r = h // g
    q = SSD_CHUNK
    nc = l // q
    dtype = x.dtype
    xc = x.reshape(bsz, nc, q, g, r, p)
    bc = bm.reshape(bsz, nc, q, g, n)
    cc = cm.reshape(bsz, nc, q, g, n)
    a_cs = jnp.cumsum(a.astype(jnp.float32).reshape(bsz, nc, q, g, r).transpose(0, 3, 4, 1, 2), axis=-1)
    lower = jnp.tril(jnp.ones((q, q), dtype=bool))
    seg = jnp.exp(jnp.where(lower, a_cs[..., :, None] - a_cs[..., None, :], -jnp.inf)).astype(dtype)
    cb = jnp.einsum('bclgn,bcsgn->bcgls', cc, bc)
    y_diag = jnp.einsum('bcgls,bgrcls,bcsgrp->bclgrp', cb, seg, xc)
    to_end = jnp.exp(a_cs[..., -1:] - a_cs).astype(dtype)
    chunk_states = jnp.einsum('bcsgn,bgrcs,bcsgrp->cbgrpn', bc, to_end, xc)
    chunk_decay = jnp.exp(a_cs[..., -1]).astype(dtype).transpose(3, 0, 1, 2)

    def step(state, inp):
        s_c, dec = inp
        return state * dec[..., None, None] + s_c, state

    h_final, h_in = lax.scan(step, h0.astype(dtype), (chunk_states, chunk_decay))
    from_start = jnp.exp(a_cs).astype(dtype)
    y_off = jnp.einsum('bclgn,cbgrpn,bgrcl->bclgrp', cc, h_in, from_start)
    return (y_diag + y_off).reshape(bsz, l, h, p), h_final


def bidirectional_ssd(xbc, dtr, xbc_c, dtr_c, a_log, dt_bias, d_skip):
    def parts(t):
        xs, bm, cm = split_cols(t, (SSM_INNER, SSM_GROUPS * SSM_STATE, SSM_GROUPS * SSM_STATE))
        b_, l_ = t.shape[:2]
        return (xs.reshape(b_, l_, SSM_HEADS, SSM_HEAD_DIM),
                bm.reshape(b_, l_, SSM_GROUPS, SSM_STATE),
                cm.reshape(b_, l_, SSM_GROUPS, SSM_STATE))

    lat = parts(xbc)
    ctxp = parts(xbc_c)
    h_zero = jnp.zeros((xbc_c.shape[0], SSM_GROUPS, SSM_HEADS // SSM_GROUPS, SSM_HEAD_DIM, SSM_STATE), xbc.dtype)

    def run(pp, dt_raw, h0, d):
        xs, bm, cm = pp
        dt = jax.nn.softplus(dt_raw[:, :, d].astype(jnp.float32) + dt_bias[d].astype(jnp.float32))
        a = -dt * jnp.exp(a_log[d].astype(jnp.float32))
        xdt = xs * dt[..., None].astype(xs.dtype)
        order = (lambda t: jnp.flip(t, axis=1)) if d == 1 else (lambda t: t)
        y, h_last = ssd_scan(order(xdt), order(a), order(bm), order(cm), h0)
        return order(y) + d_skip[d][:, None] * xs, h_last

    y_lat = 0
    y_ctx = 0
    for d in range(2):
        yc, h_ctx = run(ctxp, dtr_c, h_zero, d)
        yl, _ = run(lat, dtr, h_ctx, d)
        y_lat = y_lat + yl
        y_ctx = y_ctx + yc
    return y_lat, y_ctx


def gated_rmsnorm(y, z, g):
    yz = (y * jax.nn.silu(z)).astype(jnp.float32)
    shp = yz.shape
    yz = yz.reshape(shp[:-1] + (SSM_GROUPS, shp[-1] // SSM_GROUPS))
    yz = yz * lax.rsqrt(jnp.mean(yz * yz, axis=-1, keepdims=True) + RMS_EPS)
    return yz.reshape(shp).astype(y.dtype) * g


def attn_ssd_mixer(h, hc, w_in, w_out, rpb, conv_w, conv_b, a_log, dt_bias, d_skip, norm_g, ctx_out):
    bsz, s, _ = h.shape
    l = hc.shape[1]
    heads = lambda t: t.reshape(t.shape[0], t.shape[1], NA_HEADS, NA_HEAD_DIM)
    q, z, k, v, xbc, dtr = split_cols(h @ w_in, IN_SPLITS)
    if ctx_out:
        qc, zc, kc, vc, xbc_c, dtr_c = split_cols(hc @ w_in, IN_SPLITS)
    else:
        kc, vc, xbc_c, dtr_c = split_cols(hc @ w_in[:, CTX_KV_OFFSET:], IN_SPLITS[2:])
    att = neighbourhood_attention(heads(q), heads(k), heads(v), heads(kc), heads(vc), rpb)
    xbc = jax.nn.silu(dwconv_centred(xbc, conv_w, conv_b))
    xbc_c = jax.nn.silu(dwconv_centred(xbc_c, conv_w, conv_b))
    y_ssm, y_ssm_c = bidirectional_ssd(xbc, dtr.reshape(bsz, s, 2, SSM_HEADS), xbc_c,
                                       dtr_c.reshape(bsz, l, 2, SSM_HEADS), a_log, dt_bias, d_skip)
    ssm = gated_rmsnorm(y_ssm.reshape(bsz, s, SSM_INNER), z, norm_g)
    y = jnp.concatenate([att, ssm], axis=-1) @ w_out
    if not ctx_out:
        return y, None
    att_c = context_attention(heads(qc), heads(kc), heads(vc))
    ssm_c = gated_rmsnorm(y_ssm_c.reshape(bsz, l, SSM_INNER), zc, norm_g)
    return y, jnp.concatenate([att_c, ssm_c], axis=-1) @ w_out


def multiscale_pool_mixer(h, pool_w, pool_b, pool_scale):
    bsz, l, d = h.shape
    hf = h.astype(jnp.float32)
    cs = jnp.concatenate([jnp.zeros((bsz, 1, d), jnp.float32), jnp.cumsum(hf, axis=1)], axis=1)
    t = jnp.arange(l)
    outs = []
    for gi, w in enumerate(POOL_WINDOWS):
        lo = jnp.clip(t - w // 2, 0, l)
        hi = jnp.clip(t - w // 2 + w, 0, l)
        sl = slice(gi * POOL_GROUP, (gi + 1) * POOL_GROUP)
        csg = cs[..., sl]
        mean = (csg[:, hi] - csg[:, lo]) / (hi - lo).astype(jnp.float32)[:, None]
        outs.append(mean - hf[..., sl])
    pooled = jnp.stack(outs, axis=2).astype(h.dtype)
    y = jnp.einsum('blgc,gcd->blgd', pooled, pool_w) + pool_b
    return y.reshape(bsz, l, d) * pool_scale


def conv_ffn(h, w_up, conv_w, conv_b, w_down):
    u, v = jnp.split(h @ w_up, 2, axis=-1)
    return (jax.nn.gelu(dwconv_centred(u, conv_w, conv_b), approximate=False) * v) @ w_down


def setup_inputs(seed: int = 0) -> dict:
    key = jax.random.key(seed)
    ks = jax.random.split(key, 24)
    f32 = jnp.float32

    def nrm(k, shape, s):
        return jax.random.normal(k, shape, f32) * s

    dt0 = jnp.exp(jax.random.uniform(ks[13], (N_EVEN, 2, SSM_HEADS), f32, math.log(DT_MIN), math.log(DT_MAX)))
    return {
        'x': nrm(ks[0], (BATCH, SEQ, D_MODEL), 1.0),
        'c': nrm(ks[1], (BATCH, D_MODEL), 1.0),
        'ctx': nrm(ks[2], (BATCH, CTX_LEN, D_MODEL), 1.0),
        'c_ctx': nrm(ks[3], (D_MODEL,), 1.0),
        'ada_w': nrm(ks[4], (DEPTH, D_MODEL, 6 * D_MODEL), D_MODEL ** -0.5),
        'ada_b': nrm(ks[5], (DEPTH, 6 * D_MODEL), 0.01),
        'norm_g': 1.0 + nrm(ks[6], (DEPTH, 4, D_MODEL), 0.05),
        'w_in': nrm(ks[7], (N_EVEN, D_MODEL, IN_WIDTH), D_MODEL ** -0.5),
        'w_out': nrm(ks[8], (N_EVEN, MIX_WIDTH, D_MODEL), MIX_WIDTH ** -0.5),
        'na_rpb': nrm(ks[9], (N_EVEN, NA_HEADS, 2 * NA_WIN_ROWS - 1, 2 * NA_WIN_COLS - 1), 0.1),
        'ssm_conv_w': nrm(ks[10], (N_EVEN, SSM_CONV, SSM_CONV_DIM), SSM_CONV ** -0.5),
        'ssm_conv_b': nrm(ks[11], (N_EVEN, SSM_CONV_DIM), 0.01),
        'ssm_a_log': jnp.log(jax.random.uniform(ks[12], (N_EVEN, 2, SSM_HEADS), f32, 1.0, 16.0)),
        'ssm_dt_bias': dt0 + jnp.log(-jnp.expm1(-dt0)),
        'ssm_d': 1.0 + nrm(ks[14], (N_EVEN, 2, SSM_HEADS), 0.1),
        'ssm_norm_g': 1.0 + nrm(ks[15], (N_EVEN, SSM_INNER), 0.05),
        'pool_w': nrm(ks[16], (N_ODD, N_POOL, POOL_GROUP, POOL_GROUP), POOL_GROUP ** -0.5),
        'pool_b': nrm(ks[17], (N_ODD, N_POOL, POOL_GROUP), 0.01),
        'pool_scale': 1.0 + nrm(ks[18], (N_ODD, D_MODEL), 0.1),
        'ffn_w_up': nrm(ks[19], (DEPTH, D_MODEL, 2 * FFN_HIDDEN), D_MODEL ** -0.5),
        'ffn_conv_w': nrm(ks[20], (DEPTH, FFN_CONV, FFN_HIDDEN), FFN_CONV ** -0.5),
        'ffn_conv_b': nrm(ks[21], (DEPTH, FFN_HIDDEN), 0.01),
        'ffn_w_down': nrm(ks[22], (DEPTH, FFN_HIDDEN, D_MODEL), FFN_HIDDEN ** -0.5),
    }


def reference(x, c, ctx, c_ctx, ada_w, ada_b, norm_g, w_in, w_out, na_rpb, ssm_conv_w, ssm_conv_b,
              ssm_a_log, ssm_dt_bias, ssm_d, ssm_norm_g, pool_w, pool_b, pool_scale,
              ffn_w_up, ffn_conv_w, ffn_conv_b, ffn_w_down):
    xc = ctx
    for i in range(DEPTH):
        ctx_live = any(j % 2 == 0 for j in range(i + 1, DEPTH))
        m = (jax.nn.silu(c) @ ada_w[i] + ada_b[i])[:, None, :]
        sh1, sc1, g1, sh2, sc2, g2 = jnp.split(m, 6, axis=-1)
        mc = jax.nn.silu(c_ctx) @ ada_w[i] + ada_b[i]
        csh1, csc1, cg1, csh2, csc2, cg2 = jnp.split(mc, 6, axis=-1)
        g_pre_mix, g_post_mix, g_pre_ffn, g_post_ffn = norm_g[i]

        h = modulate(rmsnorm(x, g_pre_mix), sh1, sc1)
        if i % 2 == 0:
            e = i // 2
            hc = modulate(rmsnorm(xc, g_pre_mix), csh1, csc1)
            y, yc = attn_ssd_mixer(h, hc, w_in[e], w_out[e], na_rpb[e], ssm_conv_w[e], ssm_conv_b[e],
                                   ssm_a_log[e], ssm_dt_bias[e], ssm_d[e], ssm_norm_g[e], ctx_live)
        else:
            o = i // 2
            y = multiscale_pool_mixer(h, pool_w[o], pool_b[o], pool_scale[o])
            yc = None
            if ctx_live:
                hc = modulate(rmsnorm(xc, g_pre_mix), csh1, csc1)
                yc = multiscale_pool_mixer(hc, pool_w[o], pool_b[o], pool_scale[o])
        x = x + g1 * rmsnorm(y, g_post_mix)
        hf = modulate(rmsnorm(x, g_pre_ffn), sh2, sc2)
        x = x + g2 * rmsnorm(conv_ffn(hf, ffn_w_up[i], ffn_conv_w[i], ffn_conv_b[i], ffn_w_down[i]), g_post_ffn)
        if ctx_live:
            xc = xc + cg1 * rmsnorm(yc, g_post_mix)
            hfc = modulate(rmsnorm(xc, g_pre_ffn), csh2, csc2)
            xc = xc + cg2 * rmsnorm(conv_ffn(hfc, ffn_w_up[i], ffn_conv_w[i], ffn_conv_b[i], ffn_w_down[i]), g_post_ffn)
    return x
```

```python
import functools

import numpy as np
import jax
import jax.numpy as jnp
from jax import lax
from jax.experimental import pallas as pl
from jax.experimental.pallas import tpu as pltpu

F32 = jnp.float32
BF16 = jnp.bfloat16

GRID_W = 64
NA_HEADS = 8
NA_HEAD_DIM = 64
NA_WIDTH = NA_HEADS * NA_HEAD_DIM
NA_WIN_ROWS = 8
NA_WIN_COLS = 16
SSM_HEADS = 16
SSM_HEAD_DIM = 64
SSM_INNER = SSM_HEADS * SSM_HEAD_DIM
SSM_GROUPS = 4
SSM_STATE = 128
SSD_CHUNK = 128
SSM_BC = SSM_GROUPS * SSM_STATE
SSM_CONV_DIM = SSM_INNER + 2 * SSM_BC
POOL_WINDOWS = (2, 4, 8, 16)
RMS_EPS = 1e-6

LANES = 128
HALO = 16
NA_QROWS = 8
NA_KROWS = 16
NEG = -1e30
VMEM_LIMIT = 56 * 1024 * 1024


def _cparams(n_axes):
    return pltpu.CompilerParams(dimension_semantics=("arbitrary",) * n_axes, vmem_limit_bytes=VMEM_LIMIT)


def _rms(x, g):
    return x * lax.rsqrt(jnp.mean(x * x, axis=-1, keepdims=True) + RMS_EPS) * g


def _silu(x):
    return x * jax.nn.sigmoid(x)


def _dot(a, b):
    return jnp.dot(a, b, preferred_element_type=F32)


def _dot_nt(a, b):
    return lax.dot_general(a, b, (((1,), (1,)), ((), ())), preferred_element_type=F32)


def _split3(a):
    hi = a.astype(BF16)
    r = a - hi.astype(F32)
    mid = r.astype(BF16)
    lo = (r - mid.astype(F32)).astype(BF16)
    return hi, mid, lo


def _ada_kernel(c_ref, w_ref, b_ref, o_ref):
    s = _silu(c_ref[...])
    w = w_ref[0]
    acc = jnp.zeros(o_ref.shape[1:], F32)
    s_parts = _split3(s)
    w_parts = _split3(w)
    for si, sp in enumerate(s_parts):
        for wi, wp in enumerate(w_parts):
            if si + wi <= 2:
                acc = acc + _dot(sp, wp)
    o_ref[0] = acc + b_ref[0]


def _ada_call(c_ext, ada_w, ada_b):
    depth, d, n = ada_w.shape
    r = c_ext.shape[0]
    tn = 768
    return pl.pallas_call(
        _ada_kernel,
        grid=(depth, n // tn),
        in_specs=[
            pl.BlockSpec((r, d), lambda l, j: (0, 0)),
            pl.BlockSpec((1, d, tn), lambda l, j: (l, 0, j)),
            pl.BlockSpec((1, 1, tn), lambda l, j: (l, 0, j)),
        ],
        out_specs=pl.BlockSpec((1, r, tn), lambda l, j: (l, 0, j)),
        out_shape=jax.ShapeDtypeStruct((depth, r, n), F32),
        compiler_params=_cparams(2),
        name="ada_mod",
    )(c_ext, ada_w, ada_b.reshape(depth, 1, n))


def _inproj_kernel(x_ref, ctx_ref, sh_ref, sc_ref, csh_ref, csc_ref, g_ref, w_ref,
                   q_ref, z_ref, k_ref, v_ref, xbc_ref, dt_ref):
    i = pl.program_id(1)
    is_ctx = i == 0
    xt = jnp.where(is_ctx, ctx_ref[0], x_ref[0])
    sh = jnp.where(is_ctx, csh_ref[0], sh_ref[0])
    sc = jnp.where(is_ctx, csc_ref[0], sc_ref[0])
    hb = (_rms(xt, g_ref[...]) * (1.0 + sc) + sh).astype(BF16)

    @pl.when(i > 0)
    def _():
        q_ref[0] = _dot(hb, w_ref[:, 0:NA_WIDTH]).astype(q_ref.dtype)
        z_ref[0] = _dot(hb, w_ref[:, NA_WIDTH:NA_WIDTH + SSM_INNER]).astype(z_ref.dtype)

    o = NA_WIDTH + SSM_INNER
    k_ref[0] = _dot(hb, w_ref[:, o:o + NA_WIDTH]).astype(k_ref.dtype)
    o += NA_WIDTH
    v_ref[0] = _dot(hb, w_ref[:, o:o + NA_WIDTH]).astype(v_ref.dtype)
    o += NA_WIDTH
    xbc_ref[0] = _dot(hb, w_ref[:, o:o + SSM_CONV_DIM]).astype(xbc_ref.dtype)
    o += SSM_CONV_DIM
    dt_ref[0] = _dot(hb, w_ref[:, o:o + LANES]).astype(dt_ref.dtype)


def _inproj_call(x, ctx, sh, sc, csh, csc, g, w):
    b, s, d = x.shape
    l = ctx.shape[1]
    tm = l
    ns = s // tm
    n = w.shape[1]
    lat = lambda bb, i: (bb, jnp.maximum(i - 1, 0), 0)
    allt = lambda bb, i: (bb, jnp.where(i == 0, ns, i - 1), 0)
    per_b = lambda bb, i: (bb, 0, 0)
    fixed3 = lambda bb, i: (0, 0, 0)
    return pl.pallas_call(
        _inproj_kernel,
        grid=(b, ns + 1),
        in_specs=[
            pl.BlockSpec((1, tm, d), lat),
            pl.BlockSpec((1, l, d), per_b),
            pl.BlockSpec((1, 1, d), per_b),
            pl.BlockSpec((1, 1, d), per_b),
            pl.BlockSpec((1, 1, d), fixed3),
            pl.BlockSpec((1, 1, d), fixed3),
            pl.BlockSpec((1, d), lambda bb, i: (0, 0)),
            pl.BlockSpec((d, n), lambda bb, i: (0, 0)),
        ],
        out_specs=[
            pl.BlockSpec((1, tm, NA_WIDTH), lat),
            pl.BlockSpec((1, tm, SSM_INNER), lat),
            pl.BlockSpec((1, tm, NA_WIDTH), allt),
            pl.BlockSpec((1, tm, NA_WIDTH), allt),
            pl.BlockSpec((1, tm, SSM_CONV_DIM), allt),
            pl.BlockSpec((1, tm, LANES), allt),
        ],
        out_shape=[
            jax.ShapeDtypeStruct((b, s, NA_WIDTH), BF16),
            jax.ShapeDtypeStruct((b, s, SSM_INNER), F32),
            jax.ShapeDtypeStruct((b, s + l, NA_WIDTH), BF16),
            jax.ShapeDtypeStruct((b, s + l, NA_WIDTH), BF16),
            jax.ShapeDtypeStruct((b, s + l, SSM_CONV_DIM), F32),
            jax.ShapeDtypeStruct((b, s + l, LANES), F32),
        ],
        compiler_params=_cparams(2),
        name="inproj",
    )(x, ctx, sh, sc, csh, csc, g, w)


def _conv_silu_kernel(x_ref, prev_ref, next_ref, w_ref, b_ref, o_ref, *, ns):
    i = pl.program_id(1)
    x = x_ref[0]
    ts = x.shape[0]
    has_prev = jnp.logical_and(i != 0, i != ns)
    has_next = i < ns - 1
    p = jnp.where(has_prev, prev_ref[0, 7:8, :], 0.0)
    n = jnp.where(has_next, next_ref[0, 0:1, :], 0.0)
    row = lax.broadcasted_iota(jnp.int32, (ts, 1), 0)
    xm = jnp.where(row == 0, p, pltpu.roll(x, 1, 0))
    xp = jnp.where(row == ts - 1, n, pltpu.roll(x, ts - 1, 0))
    y = xm * w_ref[0:1, :] + x * w_ref[1:2, :] + xp * w_ref[2:3, :] + b_ref[...]
    o_ref[0] = _silu(y).astype(o_ref.dtype)


def _conv_silu_call(xbc, w, bias, ts, ns):
    b, t, c = xbc.shape
    nt = t // ts
    r8 = ts // 8
    return pl.pallas_call(
        functools.partial(_conv_silu_kernel, ns=ns),
        grid=(b, nt),
        in_specs=[
            pl.BlockSpec((1, ts, c), lambda bb, i: (bb, i, 0)),
            pl.BlockSpec((1, 8, c), lambda bb, i: (bb, jnp.maximum(i * r8 - 1, 0), 0)),
            pl.BlockSpec((1, 8, c), lambda bb, i: (bb, jnp.minimum((i + 1) * r8, t // 8 - 1), 0)),
            pl.BlockSpec((3, c), lambda bb, i: (0, 0)),
            pl.BlockSpec((1, c), lambda bb, i: (0, 0)),
        ],
        out_specs=pl.BlockSpec((1, ts, c), lambda bb, i: (bb, i, 0)),
        out_shape=jax.ShapeDtypeStruct((b, t, c), BF16),
        compiler_params=_cparams(2),
        name="ssm_conv_silu",
    )(xbc, xbc, xbc, w, bias)


def _na_bias_tables(rpb, rows):
    h = rpb.shape[0]
    w = GRID_W
    n_rel_c = 2 * NA_WIN_COLS - 1
    p = jnp.pad(rpb, ((0, 0), (0, 0), (0, 2 * w - n_rel_c)))
    f = jnp.tile(p, (1, 1, w))
    t = f[..., NA_WIN_COLS - 1:NA_WIN_COLS - 1 + w * (2 * w - 1)].reshape(h, -1, w, 2 * w - 1)[..., :w]
    i = np.arange(NA_QROWS)[:, None]
    j = np.arange(NA_KROWS)[None, :]
    c = np.arange(w)[:, None]
    kc = np.arange(w)[None, :]
    cs = np.clip(c - NA_WIN_COLS // 2, 0, w - NA_WIN_COLS)
    col_ok = (kc >= cs) & (kc < cs + NA_WIN_COLS)
    tabs = []
    for r0, k0 in ((0, 0), (NA_QROWS, NA_QROWS - NA_WIN_ROWS // 2), (rows - NA_QROWS, rows - NA_KROWS)):
        r = r0 + i
        kr = k0 + j
        rs = np.clip(r - NA_WIN_ROWS // 2, 0, rows - NA_WIN_ROWS)
        row_ok = (kr >= rs) & (kr < rs + NA_WIN_ROWS)
        dr = np.clip(kr - r + NA_WIN_ROWS - 1, 0, 2 * NA_WIN_ROWS - 2)
        blocks = jnp.take(t, jnp.asarray(dr.reshape(-1)), axis=1)
        blocks = blocks.reshape(h, NA_QROWS, NA_KROWS, w, w).transpose(0, 1, 3, 2, 4)
        ok = (row_ok[:, None, :, None] & col_ok[None, :, None, :])
        tab = jnp.where(jnp.asarray(ok)[None], blocks, NEG)
        tabs.append(tab.reshape(h, NA_QROWS * w, NA_KROWS * w))
    return jnp.stack(tabs, axis=1)


def _na_kernel(q_ref, k_ref, v_ref, bias_ref, o_ref, *, rows, n_lat):
    rb = pl.program_id(2)
    nrb = pl.num_programs(2)
    var = jnp.where(rb == 0, 0, jnp.where(rb == nrb - 1, 2, 1))
    k0 = jnp.clip(rb * NA_QROWS - NA_WIN_ROWS // 2, 0, rows - NA_KROWS)
    kstart = pl.multiple_of(k0 * GRID_W, GRID_W)
    nk = NA_KROWS * GRID_W
    q = q_ref[0]
    kw = k_ref[0, pl.ds(kstart, nk), :]
    vw = v_ref[0, pl.ds(kstart, nk), :]
    kc = k_ref[0, n_lat:, :]
    vc = v_ref[0, n_lat:, :]
    lane = lax.broadcasted_iota(jnp.int32, (1, LANES), 1)
    scale = NA_HEAD_DIM ** -0.5
    out = jnp.zeros(o_ref.shape[1:], F32)
    for hh in range(LANES // NA_HEAD_DIM):
        sel = jnp.logical_and(lane >= hh * NA_HEAD_DIM, lane < (hh + 1) * NA_HEAD_DIM)
        qm = jnp.where(sel, q, jnp.zeros_like(q))
        s = _dot_nt(qm, kw) * scale + bias_ref[hh, var]
        sc = _dot_nt(qm, kc) * scale
        m = jnp.maximum(jnp.max(s, axis=-1, keepdims=True), jnp.max(sc, axis=-1, keepdims=True))
        p = jnp.exp(s - m)
        pc = jnp.exp(sc - m)
        den = jnp.sum(p, axis=-1, keepdims=True) + jnp.sum(pc, axis=-1, keepdims=True)
        o = _dot(p.astype(BF16), vw) + _dot(pc.astype(BF16), vc)
        out = jnp.where(sel, o / den, out)
    o_ref[0] = out.astype(o_ref.dtype)


def _na_call(q, k, v, bias, n_lat):
    b, s, _ = q.shape
    t = k.shape[1]
    rows = s // GRID_W
    nq = NA_QROWS * GRID_W
    nk = NA_KROWS * GRID_W
    hp = LANES // NA_HEAD_DIM
    return pl.pallas_call(
        functools.partial(_na_kernel, rows=rows, n_lat=n_lat),
        grid=(NA_WIDTH // LANES, b, rows // NA_QROWS),
        in_specs=[
            pl.BlockSpec((1, nq, LANES), lambda p, bb, rb: (bb, rb, p)),
            pl.BlockSpec((1, t, LANES), lambda p, bb, rb: (bb, 0, p)),
            pl.BlockSpec((1, t, LANES), lambda p, bb, rb: (bb, 0, p)),
            pl.BlockSpec((hp, 3, nq, nk), lambda p, bb, rb: (p, 0, 0, 0)),
        ],
        out_specs=pl.BlockSpec((1, nq, LANES), lambda p, bb, rb: (bb, rb, p)),
        out_shape=jax.ShapeDtypeStruct((b, s, NA_WIDTH), BF16),
        compiler_params=_cparams(3),
        name="nbr_attention",
    )(q, k, v, bias)


def _ssd_direction(xbc_ref, dt_ref, prm_ref, dsk_ref, state_ref, y_ref, *, reverse, emit_y):
    q = SSD_CHUNK
    lane_off = SSM_HEADS if reverse else 0
    dt = jax.nn.softplus(dt_ref[0] + prm_ref[0:1, :])
    a = -dt * jnp.exp(prm_ref[1:2, :])
    ri = lax.broadcasted_iota(jnp.int32, (q, q), 0)
    ci = lax.broadcasted_iota(jnp.int32, (q, q), 1)
    tri = (ci >= ri) if reverse else (ci <= ri)
    ones = jnp.where(tri, 1.0, 0.0).astype(BF16)
    cs = jnp.zeros((q, LANES), F32)
    for part in _split3(a):
        cs = cs + _dot(ones, part)
    tot_row = cs[q - 1:q, :] if not reverse else cs[0:1, :]
    cs_t = cs.T
    dt_t = dt.T
    e_end_t = jnp.exp(tot_row.T - cs_t) * dt_t
    e_start = jnp.exp(cs)
    dec = jnp.exp(tot_row)

    lane = lax.broadcasted_iota(jnp.int32, (1, LANES), 1)
    lo = lane < SSM_HEAD_DIM
    for g in range(SSM_GROUPS):
        bm = xbc_ref[0, :, SSM_INNER + g * SSM_STATE:SSM_INNER + (g + 1) * SSM_STATE]
        cm = xbc_ref[0, :, SSM_INNER + SSM_BC + g * SSM_STATE:SSM_INNER + SSM_BC + (g + 1) * SSM_STATE]
        bt = bm.astype(F32).T
        if emit_y:
            cb = _dot_nt(cm, bm)
            cmf = cm.astype(F32)
        heads_per_group = SSM_HEADS // SSM_GROUPS
        for pp in range(heads_per_group // 2):
            pair = g * (heads_per_group // 2) + pp
            xs = xbc_ref[0, :, pair * LANES:(pair + 1) * LANES]
            xs_lo = jnp.where(lo, xs, jnp.zeros_like(xs))
            xs_hi = jnp.where(lo, jnp.zeros_like(xs), xs)
            rhs_x = jnp.concatenate([xs_lo, xs_hi], axis=0)
            st = state_ref[pair]
            h0 = lane_off + 2 * pair
            btw = [bt * e_end_t[h0 + u:h0 + u + 1, :] for u in range(2)]
            upd = _dot(jnp.concatenate(btw, axis=1).astype(BF16), rhs_x)
            if emit_y:
                stb = st.astype(BF16)
                st_lo = jnp.where(lo, stb, jnp.zeros_like(stb))
                st_hi = jnp.where(lo, jnp.zeros_like(stb), stb)
                lhs = []
                for u in range(2):
                    hcol = cs[:, h0 + u:h0 + u + 1]
                    hrow = cs_t[h0 + u:h0 + u + 1, :]
                    seg = jnp.exp(jnp.where(tri, hcol - hrow, NEG))
                    lhs.append(cb * seg * dt_t[h0 + u:h0 + u + 1, :])
                for u in range(2):
                    lhs.append(cmf * e_start[:, h0 + u:h0 + u + 1])
                y = _dot(jnp.concatenate(lhs, axis=1).astype(BF16),
                         jnp.concatenate([rhs_x, st_lo, st_hi], axis=0))
                y = y + dsk_ref[:, pair * LANES:(pair + 1) * LANES] * xs.astype(F32)
                y_ref[0, :, pair * LANES:(pair + 1) * LANES] = y.astype(y_ref.dtype)
            dpair = jnp.where(lo, dec[:, h0:h0 + 1], dec[:, h0 + 1:h0 + 2])
            state_ref[pair] = st * dpair + upd


def _ssd_kernel(xf_ref, dtf_ref, xb_ref, dtb_ref, prm_ref, dsk_ref, yf_ref, yb_ref, sf_ref, sb_ref, *, n_ctx):
    s = pl.program_id(1)

    @pl.when(s == 0)
    def _():
        sf_ref[...] = jnp.zeros_like(sf_ref)
        sb_ref[...] = jnp.zeros_like(sb_ref)

    def run(emit_y):
        _ssd_direction(xf_ref, dtf_ref, prm_ref, dsk_ref.at[0:1], sf_ref, yf_ref, reverse=False, emit_y=emit_y)
        _ssd_direction(xb_ref, dtb_ref, prm_ref, dsk_ref.at[1:2], sb_ref, yb_ref, reverse=True, emit_y=emit_y)

    @pl.when(s < n_ctx)
    def _():
        run(False)

    @pl.when(s >= n_ctx)
    def _():
        run(True)


def _ssd_call(xbc, dt, prm, dsk, n_lat):
    b, t, c = xbc.shape
    q = SSD_CHUNK
    nc = t // q
    nl = n_lat // q
    n_ctx = nc - nl
    fwd = lambda bb, s: (bb, jnp.where(s < n_ctx, nl + s, s - n_ctx), 0)
    bwd = lambda bb, s: (bb, nc - 1 - s, 0)
    yf = lambda bb, s: (bb, jnp.maximum(s - n_ctx, 0), 0)
    yb = lambda bb, s: (bb, jnp.minimum(nc - 1 - s, nl - 1), 0)
    fixed = lambda bb, s: (0, 0)
    return pl.pallas_call(
        functools.partial(_ssd_kernel, n_ctx=n_ctx),
        grid=(b, nc),
        in_specs=[
            pl.BlockSpec((1, q, c), fwd),
            pl.BlockSpec((1, q, LANES), fwd),
            pl.BlockSpec((1, q, c), bwd),
            pl.BlockSpec((1, q, LANES), bwd),
            pl.BlockSpec((2, LANES), fixed),
            pl.BlockSpec((2, SSM_INNER), fixed),
        ],
        out_specs=[
            pl.BlockSpec((1, q, SSM_INNER), yf),
            pl.BlockSpec((1, q, SSM_INNER), yb),
        ],
        out_shape=[jax.ShapeDtypeStruct((b, n_lat, SSM_INNER), F32)] * 2,
        scratch_shapes=[pltpu.VMEM((SSM_HEADS // 2, SSM_STATE, LANES), F32)] * 2,
        compiler_params=_cparams(2),
        name="ssd_scan",
    )(xbc, dt, xbc, dt, prm, dsk)


def _outproj_kernel(x_ref, att_ref, yf_ref, yb_ref, z_ref, gate_ref, ng_ref, gpost_ref, w_ref, o_ref):
    y = yf_ref[0] + yb_ref[0]
    yz = y * _silu(z_ref[0])
    gw = SSM_INNER // SSM_GROUPS
    parts = []
    for g in range(SSM_GROUPS):
        seg = yz[:, g * gw:(g + 1) * gw]
        parts.append(seg * lax.rsqrt(jnp.mean(seg * seg, axis=-1, keepdims=True) + RMS_EPS))
    ssm = (jnp.concatenate(parts, axis=-1) * ng_ref[...]).astype(BF16)
    o = _dot(att_ref[0], w_ref[0:NA_WIDTH, :]) + _dot(ssm, w_ref[NA_WIDTH:, :])
    o_ref[0] = x_ref[0] + gate_ref[0] * _rms(o, gpost_ref[...])


def _outproj_call(x, att, yf, yb, z, gate, ng, gpost, w, tm):
    b, s, d = x.shape
    tile = lambda bb, i: (bb, i, 0)
    per_b = lambda bb, i: (bb, 0, 0)
    fixed = lambda bb, i: (0, 0)
    return pl.pallas_call(
        _outproj_kernel,
        grid=(b, s // tm),
        in_specs=[
            pl.BlockSpec((1, tm, d), tile),
            pl.BlockSpec((1, tm, NA_WIDTH), tile),
            pl.BlockSpec((1, tm, SSM_INNER), tile),
            pl.BlockSpec((1, tm, SSM_INNER), tile),
            pl.BlockSpec((1, tm, SSM_INNER), tile),
            pl.BlockSpec((1, 1, d), per_b),
            pl.BlockSpec((1, SSM_INNER), fixed),
            pl.BlockSpec((1, d), fixed),
            pl.BlockSpec(w.shape, fixed),
        ],
        out_specs=pl.BlockSpec((1, tm, d), tile),
        out_shape=jax.ShapeDtypeStruct((b, s, d), F32),
        compiler_params=_cparams(2),
        name="outproj",
    )(x, att, yf, yb, z, gate, ng, gpost, w)


def _ffn_kernel(x_ref, prev_ref, next_ref, sh_ref, sc_ref, gate_ref, gpre_ref, gpost_ref,
                wup_ref, cw_ref, cb_ref, wdn_ref, o_ref, h_scr, u_scr, *, tn):
    i = pl.program_id(1)
    nt = pl.num_programs(1)
    tm = x_ref.shape[1]
    hidden = wdn_ref.shape[0]
    x = x_ref[0]

    def nm(xt):
        return (_rms(xt, gpre_ref[...]) * (1.0 + sc_ref[0]) + sh_ref[0]).astype(BF16)

    h_scr[0:HALO, :] = nm(prev_ref[0])
    h_scr[HALO:HALO + tm, :] = nm(x)
    h_scr[HALO + tm:, :] = nm(next_ref[0])
    acc = jnp.zeros((tm, x.shape[1]), F32)
    for j in range(hidden // tn):
        u_scr[...] = _dot(h_scr[...], wup_ref[:, j * tn:(j + 1) * tn])
        v = _dot(h_scr[HALO:HALO + tm, :], wup_ref[:, hidden + j * tn:hidden + (j + 1) * tn])

        @pl.when(i == 0)
        def _():
            u_scr[HALO - 8:HALO, :] = jnp.zeros((8, tn), F32)

        @pl.when(i == nt - 1)
        def _():
            u_scr[HALO + tm:HALO + tm + 8, :] = jnp.zeros((8, tn), F32)

        cw = cw_ref[:, j * tn:(j + 1) * tn]
        cv = (u_scr[HALO - 1:HALO - 1 + tm, :] * cw[0:1] + u_scr[HALO:HALO + tm, :] * cw[1:2]
              + u_scr[HALO + 1:HALO + 1 + tm, :] * cw[2:3] + cb_ref[:, j * tn:(j + 1) * tn])
        gl = 0.5 * cv * (1.0 + lax.erf(cv * 0.7071067811865476)) * v
        acc = acc + _dot(gl.astype(BF16), wdn_ref[j * tn:(j + 1) * tn, :])
    o_ref[0] = x + gate_ref[0] * _rms(acc, gpost_ref[...])


def _ffn_call(x, sh, sc, gate, gpre, gpost, wup, cw, cb, wdn, tm, tn):
    b, s, d = x.shape
    hidden = wdn.shape[0]
    rh = tm // HALO
    tile = lambda bb, i: (bb, i, 0)
    per_b = lambda bb, i: (bb, 0, 0)
    fixed = lambda bb, i: (0, 0)
    return pl.pallas_call(
        functools.partial(_ffn_kernel, tn=tn),
        grid=(b, s // tm),
        in_specs=[
            pl.BlockSpec((1, tm, d), tile),
            pl.BlockSpec((1, HALO, d), lambda bb, i: (bb, jnp.maximum(i * rh - 1, 0), 0)),
            pl.BlockSpec((1, HALO, d), lambda bb, i: (bb, jnp.minimum((i + 1) * rh, s // HALO - 1), 0)),
            pl.BlockSpec((1, 1, d), per_b),
            pl.BlockSpec((1, 1, d), per_b),
            pl.BlockSpec((1, 1, d), per_b),
            pl.BlockSpec((1, d), fixed),
            pl.BlockSpec((1, d), fixed),
            pl.BlockSpec(wup.shape, fixed),
            pl.BlockSpec(cw.shape, fixed),
            pl.BlockSpec((1, hidden), fixed),
            pl.BlockSpec(wdn.shape, fixed),
        ],
        out_specs=pl.BlockSpec((1, tm, d), tile),
        out_shape=jax.ShapeDtypeStruct((b, s, d), F32),
        scratch_shapes=[pltpu.VMEM((tm + 2 * HALO, d), BF16), pltpu.VMEM((tm + 2 * HALO, tn), F32)],
        compiler_params=_cparams(2),
        name="conv_ffn",
    )(x, x, x, sh, sc, gate, gpre, gpost, wup, cw, cb, wdn)


def _pool_kernel(x_ref, prev_ref, next_ref, sh_ref, sc_ref, gate_ref, gpre_ref, gpost_ref,
                 pw_ref, pb_ref, ps_ref, o_ref, h_scr, *, seq):
    i = pl.program_id(1)
    nt = pl.num_programs(1)
    ts = x_ref.shape[1]
    x = x_ref[0]
    gwidth = pw_ref.shape[1]

    def nm(xt):
        return _rms(xt, gpre_ref[...]) * (1.0 + sc_ref[0]) + sh_ref[0]

    h_scr[0:HALO, :] = jnp.where(i > 0, nm(prev_ref[0]), 0.0)
    h_scr[HALO:HALO + ts, :] = nm(x)
    h_scr[HALO + ts:, :] = jnp.where(i < nt - 1, nm(next_ref[0]), 0.0)
    t = i * ts + lax.broadcasted_iota(jnp.int32, (ts, 1), 0)
    ys = []
    for gi, w in enumerate(POOL_WINDOWS):
        cols = slice(gi * gwidth, (gi + 1) * gwidth)
        tot = jnp.zeros((ts, gwidth), F32)
        for kk in range(w):
            off = HALO - w // 2 + kk
            tot = tot + h_scr[off:off + ts, cols]
        cnt = (jnp.minimum(t + w // 2, seq) - jnp.maximum(t - w // 2, 0)).astype(F32)
        pooled = tot / cnt - h_scr[HALO:HALO + ts, cols]
        ys.append(_dot(pooled.astype(BF16), pw_ref[gi]) + pb_ref[:, cols])
    y = jnp.concatenate(ys, axis=-1) * ps_ref[...]
    o_ref[0] = x + gate_ref[0] * _rms(y, gpost_ref[...])


def _pool_call(x, sh, sc, gate, gpre, gpost, pw, pb, ps, ts):
    b, s, d = x.shape
    rh = ts // HALO
    tile = lambda bb, i: (bb, i, 0)
    per_b = lambda bb, i: (bb, 0, 0)
    fixed = lambda bb, i: (0, 0)
    return pl.pallas_call(
        functools.partial(_pool_kernel, seq=s),
        grid=(b, s // ts),
        in_specs=[
            pl.BlockSpec((1, ts, d), tile),
            pl.BlockSpec((1, HALO, d), lambda bb, i: (bb, jnp.maximum(i * rh - 1, 0), 0)),
            pl.BlockSpec((1, HALO, d), lambda bb, i: (bb, jnp.minimum((i + 1) * rh, s // HALO - 1), 0)),
            pl.BlockSpec((1, 1, d), per_b),
            pl.BlockSpec((1, 1, d), per_b),
            pl.BlockSpec((1, 1, d), per_b),
            pl.BlockSpec((1, d), fixed),
            pl.BlockSpec((1, d), fixed),
            pl.BlockSpec(pw.shape, lambda bb, i: (0, 0, 0)),
            pl.BlockSpec((1, d), fixed),
            pl.BlockSpec((1, d), fixed),
        ],
        out_specs=pl.BlockSpec((1, ts, d), tile),
        out_shape=jax.ShapeDtypeStruct((b, s, d), F32),
        scratch_shapes=[pltpu.VMEM((ts + 2 * HALO, d), F32)],
        compiler_params=_cparams(2),
        name="pool_mixer",
    )(x, x, x, sh, sc, gate, gpre, gpost, pw, pb, ps)


def _mod_rows(m, b, d):
    lat = [m[:b, k * d:(k + 1) * d].reshape(b, 1, d) for k in range(6)]
    ctx = [m[b:b + 1, k * d:(k + 1) * d].reshape(1, 1, d) for k in range(6)]
    return lat, ctx


def kernel(x, c, ctx, c_ctx, ada_w, ada_b, norm_g, w_in, w_out, na_rpb, ssm_conv_w, ssm_conv_b, ssm_a_log, ssm_dt_bias, ssm_d, ssm_norm_g, pool_w, pool_b, pool_scale, ffn_w_up, ffn_conv_w, ffn_conv_b, ffn_w_down):
    b, s, d = x.shape
    l = ctx.shape[1]
    depth = ada_w.shape[0]
    hidden = ffn_w_down.shape[1]
    assert depth == 2 and s % l == 0 and l % SSD_CHUNK == 0 and s % (NA_QROWS * GRID_W) == 0
    assert s // GRID_W >= NA_KROWS and 2 * SSM_HEADS <= LANES

    rows_c = -(-(b + 1) // 8) * 8
    c_ext = jnp.zeros((rows_c, d), F32).at[:b].set(c).at[b].set(c_ctx)
    mod = _ada_call(c_ext, ada_w, ada_b)

    (sh1, sc1, g1, sh2, sc2, g2), (csh1, csc1, _, _, _, _) = _mod_rows(mod[0], b, d)
    g_pre_mix, g_post_mix, g_pre_ffn, g_post_ffn = [norm_g[0, k].reshape(1, d) for k in range(4)]
    wi = w_in[0]
    o_q, o_z, o_k, o_v, o_x, o_dt = np.cumsum((0, NA_WIDTH, SSM_INNER, NA_WIDTH, NA_WIDTH, SSM_CONV_DIM))
    w_cat = jnp.concatenate(
        [wi[:, :o_dt], jnp.pad(wi[:, o_dt:], ((0, 0), (0, LANES - 2 * SSM_HEADS)))], axis=1).astype(BF16)
    q, z, k, v, xbc_raw, dt_raw = _inproj_call(x, ctx, sh1, sc1, csh1, csc1, g_pre_mix, w_cat)

    bias = _na_bias_tables(na_rpb[0], s // GRID_W)
    att = _na_call(q, k, v, bias, s)

    xbc = _conv_silu_call(xbc_raw, ssm_conv_w[0], ssm_conv_b[0].reshape(1, -1), l, s // l)
    pad = LANES - 2 * SSM_HEADS
    prm = jnp.stack([jnp.pad(ssm_dt_bias[0].reshape(-1), (0, pad)), jnp.pad(ssm_a_log[0].reshape(-1), (0, pad))])
    dsk = jnp.repeat(ssm_d[0], SSM_HEAD_DIM, axis=1)
    y_f, y_b = _ssd_call(xbc, dt_raw, prm, dsk, s)

    x = _outproj_call(x, att, y_f, y_b, z, g1, ssm_norm_g[0].reshape(1, -1), g_post_mix,
                      w_out[0].astype(BF16), 512)
    x = _ffn_call(x, sh2, sc2, g2, g_pre_ffn, g_post_ffn, ffn_w_up[0].astype(BF16), ffn_conv_w[0],
                  ffn_conv_b[0].reshape(1, hidden), ffn_w_down[0].astype(BF16), 512, 256)

    (sh1, sc1, g1, sh2, sc2, g2), _ = _mod_rows(mod[1], b, d)
    g_pre_mix, g_post_mix, g_pre_ffn, g_post_ffn = [norm_g[1, k].reshape(1, d) for k in range(4)]
    x = _pool_call(x, sh1, sc1, g1, g_pre_mix, g_post_mix, pool_w[0].astype(BF16), pool_b[0].reshape(1, d),
                   pool_scale[0].reshape(1, d), 512)
    x = _ffn_call(x, sh2, sc2, g2, g_pre_ffn, g_post_ffn, ffn_w_up[1].astype(BF16), ffn_conv_w[1],
                  ffn_conv_b[1].reshape(1, hidden), ffn_w_down[1].astype(BF16), 512, 256)
    return x
```

```python
import functools

import numpy as np
import jax
import jax.numpy as jnp
from jax import lax
from jax.experimental import pallas as pl
from jax.experimental.pallas import tpu as pltpu

F32 = jnp.float32
BF16 = jnp.bfloat16

GRID_W = 64
NA_HEADS = 8
NA_HEAD_DIM = 64
NA_WIDTH = NA_HEADS * NA_HEAD_DIM
NA_WIN_ROWS = 8
NA_WIN_COLS = 16
SSM_HEADS = 16
SSM_HEAD_DIM = 64
SSM_INNER = SSM_HEADS * SSM_HEAD_DIM
SSM_GROUPS = 4
SSM_STATE = 128
SSD_CHUNK = 128
SSM_BC = SSM_GROUPS * SSM_STATE
SSM_CONV_DIM = SSM_INNER + 2 * SSM_BC
POOL_WINDOWS = (2, 4, 8, 16)
RMS_EPS = 1e-6

LANES = 128
HALO = 16
NA_QROWS = 8
NA_KROWS = 16
NA_DOFF = NA_QROWS
NA_NTILES = 32
NEG = -1e30
VMEM_LIMIT = 56 * 1024 * 1024


def _cparams(n_axes):
    return pltpu.CompilerParams(dimension_semantics=("arbitrary",) * n_axes, vmem_limit_bytes=VMEM_LIMIT)


def _rms(x, g):
    return x * lax.rsqrt(jnp.mean(x * x, axis=-1, keepdims=True) + RMS_EPS) * g


def _silu(x):
    return x * jax.nn.sigmoid(x)


def _dot(a, b):
    return jnp.dot(a, b, preferred_element_type=F32)


def _dot_nt(a, b):
    return lax.dot_general(a, b, (((1,), (1,)), ((), ())), preferred_element_type=F32)


def _split3(a):
    hi = a.astype(BF16)
    r = a - hi.astype(F32)
    mid = r.astype(BF16)
    lo = (r - mid.astype(F32)).astype(BF16)
    return hi, mid, lo


def _conv3_rows(u, w_ref_rows, rows):
    n = u.shape[0]
    u_prev = pltpu.roll(u, 1, 0)[HALO:HALO + rows]
    u_next = pltpu.roll(u, n - 1, 0)[HALO:HALO + rows]
    return u_prev * w_ref_rows[0:1] + u[HALO:HALO + rows] * w_ref_rows[1:2] + u_next * w_ref_rows[2:3]


def _ada_kernel(c_ref, w_ref, b_ref, o_ref):
    s = _silu(c_ref[...])
    w = w_ref[0]
    acc = jnp.zeros(o_ref.shape[1:], F32)
    s_parts = _split3(s)
    w_parts = _split3(w)
    for si, sp in enumerate(s_parts):
        for wi, wp in enumerate(w_parts):
            if si + wi <= 2:
                acc = acc + _dot(sp, wp)
    o_ref[0] = acc + b_ref[0]


def _ada_call(c_ext, ada_w, ada_b):
    depth, d, n = ada_w.shape
    r = c_ext.shape[0]
    tn = 768
    return pl.pallas_call(
        _ada_kernel,
        grid=(depth, n // tn),
        in_specs=[
            pl.BlockSpec((r, d), lambda l, j: (0, 0)),
            pl.BlockSpec((1, d, tn), lambda l, j: (l, 0, j)),
            pl.BlockSpec((1, 1, tn), lambda l, j: (l, 0, j)),
        ],
        out_specs=pl.BlockSpec((1, r, tn), lambda l, j: (l, 0, j)),
        out_shape=jax.ShapeDtypeStruct((depth, r, n), F32),
        compiler_params=_cparams(2),
        name="ada_mod",
    )(c_ext, ada_w, ada_b.reshape(depth, 1, n))


def _inproj_kernel(x_ref, prev_ref, next_ref, ctx_ref, sh_ref, sc_ref, csh_ref, csc_ref, g_ref, w_ref,
                   cw_ref, cb_ref, q_ref, z_ref, k_ref, v_ref, xbc_ref, dt_ref, h_scr):
    i = pl.program_id(1)
    ns = pl.num_programs(1) - 1
    tm = x_ref.shape[1]
    is_ctx = i == 0
    sh = jnp.where(is_ctx, csh_ref[0], sh_ref[0])
    sc = jnp.where(is_ctx, csc_ref[0], sc_ref[0])

    def nm(xt):
        return (_rms(xt, g_ref[...]) * (1.0 + sc) + sh).astype(BF16)

    zero = jnp.zeros((), BF16)
    h_scr[0:HALO, :] = jnp.where(i > 1, nm(prev_ref[0]), zero)
    h_scr[HALO:HALO + tm, :] = nm(jnp.where(is_ctx, ctx_ref[0], x_ref[0]))
    h_scr[HALO + tm:, :] = jnp.where(jnp.logical_and(i > 0, i < ns), nm(next_ref[0]), zero)
    hb = h_scr[HALO:HALO + tm, :]

    q_ref[0] = _dot(hb, w_ref[:, 0:NA_WIDTH]).astype(q_ref.dtype)
    z_ref[0] = _dot(hb, w_ref[:, NA_WIDTH:NA_WIDTH + SSM_INNER]).astype(z_ref.dtype)
    o = NA_WIDTH + SSM_INNER
    k_ref[0] = _dot(hb, w_ref[:, o:o + NA_WIDTH]).astype(k_ref.dtype)
    o += NA_WIDTH
    v_ref[0] = _dot(hb, w_ref[:, o:o + NA_WIDTH]).astype(v_ref.dtype)
    o += NA_WIDTH
    cblk = 512
    for c0 in range(0, SSM_CONV_DIM, cblk):
        u = _dot(h_scr[...], w_ref[:, o + c0:o + c0 + cblk])
        y = _conv3_rows(u, cw_ref[:, c0:c0 + cblk], tm) + cb_ref[:, c0:c0 + cblk]
        xbc_ref[0, :, c0:c0 + cblk] = _silu(y).astype(xbc_ref.dtype)
    o += SSM_CONV_DIM
    dt_ref[0] = _dot(hb, w_ref[:, o:o + LANES]).astype(dt_ref.dtype)


def _inproj_call(x, ctx, sh, sc, csh, csc, g, w, cw, cb):
    b, s, d = x.shape
    l = ctx.shape[1]
    tm = l
    ns = s // tm
    rh = tm // HALO
    n = w.shape[1]
    lat = lambda bb, i: (bb, jnp.maximum(i - 1, 0), 0)
    allt = lambda bb, i: (bb, jnp.where(i == 0, ns, i - 1), 0)
    per_b = lambda bb, i: (bb, 0, 0)
    fixed3 = lambda bb, i: (0, 0, 0)
    fixed = lambda bb, i: (0, 0)
    return pl.pallas_call(
        _inproj_kernel,
        grid=(b, ns + 1),
        in_specs=[
            pl.BlockSpec((1, tm, d), lat),
            pl.BlockSpec((1, HALO, d), lambda bb, i: (bb, jnp.maximum((i - 1) * rh - 1, 0), 0)),
            pl.BlockSpec((1, HALO, d), lambda bb, i: (bb, jnp.minimum(i * rh, s // HALO - 1), 0)),
            pl.BlockSpec((1, l, d), per_b),
            pl.BlockSpec((1, 1, d), per_b),
            pl.BlockSpec((1, 1, d), per_b),
            pl.BlockSpec((1, 1, d), fixed3),
            pl.BlockSpec((1, 1, d), fixed3),
            pl.BlockSpec((1, d), fixed),
            pl.BlockSpec((d, n), fixed),
            pl.BlockSpec(cw.shape, fixed),
            pl.BlockSpec(cb.shape, fixed),
        ],
        out_specs=[
            pl.BlockSpec((1, tm, NA_WIDTH), lat),
            pl.BlockSpec((1, tm, SSM_INNER), lat),
            pl.BlockSpec((1, tm, NA_WIDTH), allt),
            pl.BlockSpec((1, tm, NA_WIDTH), allt),
            pl.BlockSpec((1, tm, SSM_CONV_DIM), allt),
            pl.BlockSpec((1, tm, LANES), allt),
        ],
        out_shape=[
            jax.ShapeDtypeStruct((b, s, NA_WIDTH), BF16),
            jax.ShapeDtypeStruct((b, s, SSM_INNER), BF16),
            jax.ShapeDtypeStruct((b, s + l, NA_WIDTH), BF16),
            jax.ShapeDtypeStruct((b, s + l, NA_WIDTH), BF16),
            jax.ShapeDtypeStruct((b, s + l, SSM_CONV_DIM), BF16),
            jax.ShapeDtypeStruct((b, s + l, LANES), F32),
        ],
        scratch_shapes=[pltpu.VMEM((tm + 2 * HALO, d), BF16)],
        compiler_params=_cparams(2),
        name="inproj",
    )(x, x, x, ctx, sh, sc, csh, csc, g, w, cw, cb)


def _na_bias_tiles(rpb):
    h, n_rel_r, n_rel_c = rpb.shape
    w = GRID_W
    p = jnp.pad(rpb, ((0, 0), (0, 0), (0, 2 * w - n_rel_c)))
    f = jnp.tile(p, (1, 1, w))
    t = f[..., NA_WIN_COLS - 1:NA_WIN_COLS - 1 + w * (2 * w - 1)].reshape(h, n_rel_r, w, 2 * w - 1)[..., :w]
    c = np.arange(w)[:, None]
    kc = np.arange(w)[None, :]
    cs = np.clip(c - NA_WIN_COLS // 2, 0, w - NA_WIN_COLS)
    col_ok = jnp.asarray((kc >= cs) & (kc < cs + NA_WIN_COLS))
    t = jnp.where(col_ok[None, None], t, NEG)
    n_after = NA_NTILES + 1 - NA_DOFF - n_rel_r
    tz = jnp.concatenate(
        [jnp.full((h, NA_DOFF, w, w), NEG, F32), t, jnp.full((h, n_after, w, w), NEG, F32)], axis=1)
    return jnp.concatenate([tz[:, :-1], tz[:, 1:]], axis=-1)


def _na_kernel(q_ref, k_ref, v_ref, pt_ref, o_ref, *, rows, n_lat):
    rb = pl.program_id(2)
    r0 = rb * NA_QROWS
    k0 = jnp.clip(r0 - NA_WIN_ROWS // 2, 0, rows - NA_KROWS)
    kstart = pl.multiple_of(k0 * GRID_W, GRID_W)
    nk = NA_KROWS * GRID_W
    w = GRID_W
    q = q_ref[0]
    kw = k_ref[0, pl.ds(kstart, nk), :]
    vw = v_ref[0, pl.ds(kstart, nk), :]
    kc = k_ref[0, n_lat:, :]
    vc = v_ref[0, n_lat:, :]
    lane = lax.broadcasted_iota(jnp.int32, (1, LANES), 1)
    left = lane < w
    scale = jnp.asarray(NA_HEAD_DIM ** -0.5, q.dtype)
    out = jnp.zeros(o_ref.shape[1:], F32)
    for hh in range(LANES // NA_HEAD_DIM):
        sel = jnp.logical_and(lane >= hh * NA_HEAD_DIM, lane < (hh + 1) * NA_HEAD_DIM)
        qm = jnp.where(sel, q, jnp.zeros_like(q)) * scale
        s_raw = _dot_nt(qm, kw)
        row_blocks = []
        for i in range(NA_QROWS):
            lo = jnp.clip(r0 + i - NA_WIN_ROWS // 2, 0, rows - NA_WIN_ROWS) - k0
            blocks = []
            for jj in range(NA_KROWS // 2):
                t_idx = (k0 - r0) + 2 * jj - i + (NA_WIN_ROWS - 1) + NA_DOFF
                ok0 = jnp.logical_and(2 * jj >= lo, 2 * jj < lo + NA_WIN_ROWS)
                ok1 = jnp.logical_and(2 * jj + 1 >= lo, 2 * jj + 1 < lo + NA_WIN_ROWS)
                mvec = jnp.where(left, jnp.where(ok0, 0.0, NEG), jnp.where(ok1, 0.0, NEG))
                blk = s_raw[i * w:(i + 1) * w, jj * 2 * w:(jj + 1) * 2 * w]
                blocks.append(blk + pt_ref[hh, t_idx] + mvec)
            row_blocks.append(jnp.concatenate(blocks, axis=1))
        s = jnp.concatenate(row_blocks, axis=0)
        sc = _dot_nt(qm, kc)
        m = jnp.maximum(jnp.max(s, axis=-1, keepdims=True), jnp.max(sc, axis=-1, keepdims=True))
        p = jnp.exp(s - m)
        pc = jnp.exp(sc - m)
        den = jnp.sum(p, axis=-1, keepdims=True) + jnp.sum(pc, axis=-1, keepdims=True)
        o = _dot(p.astype(BF16), vw) + _dot(pc.astype(BF16), vc)
        out = jnp.where(sel, o / den, out)
    o_ref[0] = out.astype(o_ref.dtype)


def _na_call(q, k, v, pt, n_lat):
    b, s, _ = q.shape
    t = k.shape[1]
    rows = s // GRID_W
    nq = NA_QROWS * GRID_W
    hp = LANES // NA_HEAD_DIM
    return pl.pallas_call(
        functools.partial(_na_kernel, rows=rows, n_lat=n_lat),
        grid=(NA_WIDTH // LANES, b, rows // NA_QROWS),
        in_specs=[
            pl.BlockSpec((1, nq, LANES), lambda p, bb, rb: (bb, rb, p)),
            pl.BlockSpec((1, t, LANES), lambda p, bb, rb: (bb, 0, p)),
            pl.BlockSpec((1, t, LANES), lambda p, bb, rb: (bb, 0, p)),
            pl.BlockSpec((hp,) + pt.shape[1:], lambda p, bb, rb: (p, 0, 0, 0)),
        ],
        out_specs=pl.BlockSpec((1, nq, LANES), lambda p, bb, rb: (bb, rb, p)),
        out_shape=jax.ShapeDtypeStruct((b, s, NA_WIDTH), BF16),
        compiler_params=_cparams(3),
        name="nbr_attention",
    )(q, k, v, pt)


def _ssd_direction(xbc_ref, dt_ref, prm_ref, dsk_ref, state_ref, y_ref, *, reverse, emit_y):
    q = SSD_CHUNK
    lane_off = SSM_HEADS if reverse else 0
    dt = jax.nn.softplus(dt_ref[0] + prm_ref[0:1, :])
    a = -dt * jnp.exp(prm_ref[1:2, :])
    ri = lax.broadcasted_iota(jnp.int32, (q, q), 0)
    ci = lax.broadcasted_iota(jnp.int32, (q, q), 1)
    tri = (ci >= ri) if reverse else (ci <= ri)
    ones = jnp.where(tri, 1.0, 0.0).astype(BF16)
    cs = jnp.zeros((q, LANES), F32)
    for part in _split3(a):
        cs = cs + _dot(ones, part)
    tot_row = cs[q - 1:q, :] if not reverse else cs[0:1, :]
    cs_t = cs.T
    dt_t = dt.T
    e_end_t = jnp.exp(tot_row.T - cs_t) * dt_t
    e_start = jnp.exp(cs)
    dec = jnp.exp(tot_row)

    lane = lax.broadcasted_iota(jnp.int32, (1, LANES), 1)
    lo = lane < SSM_HEAD_DIM
    for g in range(SSM_GROUPS):
        bm = xbc_ref[0, :, SSM_INNER + g * SSM_STATE:SSM_INNER + (g + 1) * SSM_STATE]
        cm = xbc_ref[0, :, SSM_INNER + SSM_BC + g * SSM_STATE:SSM_INNER + SSM_BC + (g + 1) * SSM_STATE]
        bt = bm.astype(F32).T
        if emit_y:
            cb = _dot_nt(cm, bm)
            cmf = cm.astype(F32)
        heads_per_group = SSM_HEADS // SSM_GROUPS
        for pp in range(heads_per_group // 2):
            pair = g * (heads_per_group // 2) + pp
            xs = xbc_ref[0, :, pair * LANES:(pair + 1) * LANES]
            xs_lo = jnp.where(lo, xs, jnp.zeros_like(xs))
            xs_hi = jnp.where(lo, jnp.zeros_like(xs), xs)
            rhs_x = jnp.concatenate([xs_lo, xs_hi], axis=0)
            st = state_ref[pair]
            h0 = lane_off + 2 * pair
            btw = [bt * e_end_t[h0 + u:h0 + u + 1, :] for u in range(2)]
            upd = _dot(jnp.concatenate(btw, axis=1).astype(BF16), rhs_x)
            if emit_y:
                stb = st.astype(BF16)
                st_lo = jnp.where(lo, stb, jnp.zeros_like(stb))
                st_hi = jnp.where(lo, jnp.zeros_like(stb), stb)
                lhs = []
                for u in range(2):
                    hcol = cs[:, h0 + u:h0 + u + 1]
                    hrow = cs_t[h0 + u:h0 + u + 1, :]
                    seg = jnp.exp(jnp.where(tri, hcol - hrow, NEG))
                    lhs.append(cb * seg * dt_t[h0 + u:h0 + u + 1, :])
                for u in range(2):
                    lhs.append(cmf * e_start[:, h0 + u:h0 + u + 1])
                y = _dot(jnp.concatenate(lhs, axis=1).astype(BF16),
                         jnp.concatenate([rhs_x, st_lo, st_hi], axis=0))
                y = y + dsk_ref[:, pair * LANES:(pair + 1) * LANES] * xs.astype(F32)
                y_ref[0, :, pair * LANES:(pair + 1) * LANES] = y.astype(y_ref.dtype)
            dpair = jnp.where(lo, dec[:, h0:h0 + 1], dec[:, h0 + 1:h0 + 2])
            state_ref[pair] = st * dpair + upd


def _ssd_kernel(xf_ref, dtf_ref, xb_ref, dtb_ref, prm_ref, dsk_ref, yf_ref, yb_ref, sf_ref, sb_ref, *, n_ctx):
    s = pl.program_id(1)

    @pl.when(s == 0)
    def _():
        sf_ref[...] = jnp.zeros_like(sf_ref)
        sb_ref[...] = jnp.zeros_like(sb_ref)

    def run(emit_y):
        _ssd_direction(xf_ref, dtf_ref, prm_ref, dsk_ref.at[0:1], sf_ref, yf_ref, reverse=False, emit_y=emit_y)
        _ssd_direction(xb_ref, dtb_ref, prm_ref, dsk_ref.at[1:2], sb_ref, yb_ref, reverse=True, emit_y=emit_y)

    @pl.when(s < n_ctx)
    def _():
        run(False)

    @pl.when(s >= n_ctx)
    def _():
        run(True)


def _ssd_call(xbc, dt, prm, dsk, n_lat):
    b, t, c = xbc.shape
    q = SSD_CHUNK
    nc = t // q
    nl = n_lat // q
    n_ctx = nc - nl
    fwd = lambda bb, s: (bb, jnp.where(s < n_ctx, nl + s, s - n_ctx), 0)
    bwd = lambda bb, s: (bb, nc - 1 - s, 0)
    yf = lambda bb, s: (bb, jnp.maximum(s - n_ctx, 0), 0)
    yb = lambda bb, s: (bb, jnp.minimum(nc - 1 - s, nl - 1), 0)
    fixed = lambda bb, s: (0, 0)
    return pl.pallas_call(
        functools.partial(_ssd_kernel, n_ctx=n_ctx),
        grid=(b, nc),
        in_specs=[
            pl.BlockSpec((1, q, c), fwd),
            pl.BlockSpec((1, q, LANES), fwd),
            pl.BlockSpec((1, q, c), bwd),
            pl.BlockSpec((1, q, LANES), bwd),
            pl.BlockSpec((2, LANES), fixed),
            pl.BlockSpec((2, SSM_INNER), fixed),
        ],
        out_specs=[
            pl.BlockSpec((1, q, SSM_INNER), yf),
            pl.BlockSpec((1, q, SSM_INNER), yb),
        ],
        out_shape=[jax.ShapeDtypeStruct((b, n_lat, SSM_INNER), BF16)] * 2,
        scratch_shapes=[pltpu.VMEM((SSM_HEADS // 2, SSM_STATE, LANES), F32)] * 2,
        compiler_params=_cparams(2),
        name="ssd_scan",
    )(xbc, dt, xbc, dt, prm, dsk)


def _outproj_kernel(x_ref, att_ref, yf_ref, yb_ref, z_ref, gate_ref, ng_ref, gpost_ref, w_ref, o_ref):
    y = yf_ref[0].astype(F32) + yb_ref[0].astype(F32)
    yz = y * _silu(z_ref[0].astype(F32))
    gw = SSM_INNER // SSM_GROUPS
    parts = []
    for g in range(SSM_GROUPS):
        seg = yz[:, g * gw:(g + 1) * gw]
        parts.append(seg * lax.rsqrt(jnp.mean(seg * seg, axis=-1, keepdims=True) + RMS_EPS))
    ssm = (jnp.concatenate(parts, axis=-1) * ng_ref[...]).astype(BF16)
    o = _dot(att_ref[0], w_ref[0:NA_WIDTH, :]) + _dot(ssm, w_ref[NA_WIDTH:, :])
    o_ref[0] = x_ref[0] + gate_ref[0] * _rms(o, gpost_ref[...])


def _outproj_call(x, att, yf, yb, z, gate, ng, gpost, w, tm):
    b, s, d = x.shape
    tile = lambda bb, i: (bb, i, 0)
    per_b = lambda bb, i: (bb, 0, 0)
    fixed = lambda bb, i: (0, 0)
    return pl.pallas_call(
        _outproj_kernel,
        grid=(b, s // tm),
        in_specs=[
            pl.BlockSpec((1, tm, d), tile),
            pl.BlockSpec((1, tm, NA_WIDTH), tile),
            pl.BlockSpec((1, tm, SSM_INNER), tile),
            pl.BlockSpec((1, tm, SSM_INNER), tile),
            pl.BlockSpec((1, tm, SSM_INNER), tile),
            pl.BlockSpec((1, 1, d), per_b),
            pl.BlockSpec((1, SSM_INNER), fixed),
            pl.BlockSpec((1, d), fixed),
            pl.BlockSpec(w.shape, fixed),
        ],
        out_specs=pl.BlockSpec((1, tm, d), tile),
        out_shape=jax.ShapeDtypeStruct((b, s, d), F32),
        compiler_params=_cparams(2),
        name="outproj",
    )(x, att, yf, yb, z, gate, ng, gpost, w)


def _ffn_kernel(x_ref, prev_ref, next_ref, sh_ref, sc_ref, gate_ref, gpre_ref, gpost_ref,
                wup_ref, cw_ref, cb_ref, wdn_ref, o_ref, h_scr, *, tn):
    i = pl.program_id(1)
    nt = pl.num_programs(1)
    tm = x_ref.shape[1]
    hidden = wdn_ref.shape[0]
    x = x_ref[0]

    def nm(xt):
        return (_rms(xt, gpre_ref[...]) * (1.0 + sc_ref[0]) + sh_ref[0]).astype(BF16)

    zero = jnp.zeros((), BF16)
    h_scr[0:HALO, :] = jnp.where(i > 0, nm(prev_ref[0]), zero)
    h_scr[HALO:HALO + tm, :] = nm(x)
    h_scr[HALO + tm:, :] = jnp.where(i < nt - 1, nm(next_ref[0]), zero)
    acc = jnp.zeros((tm, x.shape[1]), F32)
    for j in range(hidden // tn):
        u = _dot(h_scr[...], wup_ref[:, j * tn:(j + 1) * tn])
        v = _dot(h_scr[HALO:HALO + tm, :], wup_ref[:, hidden + j * tn:hidden + (j + 1) * tn])
        cv = _conv3_rows(u, cw_ref[:, j * tn:(j + 1) * tn], tm) + cb_ref[:, j * tn:(j + 1) * tn]
        gl = 0.5 * cv * (1.0 + lax.erf(cv * 0.7071067811865476)) * v
        acc = acc + _dot(gl.astype(BF16), wdn_ref[j * tn:(j + 1) * tn, :])
    o_ref[0] = x + gate_ref[0] * _rms(acc, gpost_ref[...])


def _ffn_call(x, sh, sc, gate, gpre, gpost, wup, cw, cb, wdn, tm, tn):
    b, s, d = x.shape
    hidden = wdn.shape[0]
    rh = tm // HALO
    tile = lambda bb, i: (bb, i, 0)
    per_b = lambda bb, i: (bb, 0, 0)
    fixed = lambda bb, i: (0, 0)
    return pl.pallas_call(
        functools.partial(_ffn_kernel, tn=tn),
        grid=(b, s // tm),
        in_specs=[
            pl.BlockSpec((1, tm, d), tile),
            pl.BlockSpec((1, HALO, d), lambda bb, i: (bb, jnp.maximum(i * rh - 1, 0), 0)),
            pl.BlockSpec((1, HALO, d), lambda bb, i: (bb, jnp.minimum((i + 1) * rh, s // HALO - 1), 0)),
            pl.BlockSpec((1, 1, d), per_b),
            pl.BlockSpec((1, 1, d), per_b),
            pl.BlockSpec((1, 1, d), per_b),
            pl.BlockSpec((1, d), fixed),
            pl.BlockSpec((1, d), fixed),
            pl.BlockSpec(wup.shape, fixed),
            pl.BlockSpec(cw.shape, fixed),
            pl.BlockSpec((1, hidden), fixed),
            pl.BlockSpec(wdn.shape, fixed),
        ],
        out_specs=pl.BlockSpec((1, tm, d), tile),
        out_shape=jax.ShapeDtypeStruct((b, s, d), F32),
        scratch_shapes=[pltpu.VMEM((tm + 2 * HALO, d), BF16)],
        compiler_params=_cparams(2),
        name="conv_ffn",
    )(x, x, x, sh, sc, gate, gpre, gpost, wup, cw, cb, wdn)


def _pool_kernel(x_ref, prev_ref, next_ref, sh_ref, sc_ref, gate_ref, gpre_ref, gpost_ref,
                 pw_ref, pb_ref, ps_ref, o_ref, h_scr, *, seq):
    i = pl.program_id(1)
    nt = pl.num_programs(1)
    ts = x_ref.shape[1]
    x = x_ref[0]
    gwidth = pw_ref.shape[1]

    def nm(xt):
        return _rms(xt, gpre_ref[...]) * (1.0 + sc_ref[0]) + sh_ref[0]

    h_scr[0:HALO, :] = jnp.where(i > 0, nm(prev_ref[0]), 0.0)
    h_scr[HALO:HALO + ts, :] = nm(x)
    h_scr[HALO + ts:, :] = jnp.where(i < nt - 1, nm(next_ref[0]), 0.0)
    t = i * ts + lax.broadcasted_iota(jnp.int32, (ts, 1), 0)
    ys = []
    for gi, w in enumerate(POOL_WINDOWS):
        cols = slice(gi * gwidth, (gi + 1) * gwidth)
        tot = jnp.zeros((ts, gwidth), F32)
        for kk in range(w):
            off = HALO - w // 2 + kk
            tot = tot + h_scr[off:off + ts, cols]
        cnt = (jnp.minimum(t + w // 2, seq) - jnp.maximum(t - w // 2, 0)).astype(F32)
        pooled = tot / cnt - h_scr[HALO:HALO + ts, cols]
        ys.append(_dot(pooled.astype(BF16), pw_ref[gi]) + pb_ref[:, cols])
    y = jnp.concatenate(ys, axis=-1) * ps_ref[...]
    o_ref[0] = x + gate_ref[0] * _rms(y, gpost_ref[...])


def _pool_call(x, sh, sc, gate, gpre, gpost, pw, pb, ps, ts):
    b, s, d = x.shape
    rh = ts // HALO
    tile = lambda bb, i: (bb, i, 0)
    per_b = lambda bb, i: (bb, 0, 0)
    fixed = lambda bb, i: (0, 0)
    return pl.pallas_call(
        functools.partial(_pool_kernel, seq=s),
        grid=(b, s // ts),
        in_specs=[
            pl.BlockSpec((1, ts, d), tile),
            pl.BlockSpec((1, HALO, d), lambda bb, i: (bb, jnp.maximum(i * rh - 1, 0), 0)),
            pl.BlockSpec((1, HALO, d), lambda bb, i: (bb, jnp.minimum((i + 1) * rh, s // HALO - 1), 0)),
            pl.BlockSpec((1, 1, d), per_b),
            pl.BlockSpec((1, 1, d), per_b),
            pl.BlockSpec((1, 1, d), per_b),
            pl.BlockSpec((1, d), fixed),
            pl.BlockSpec((1, d), fixed),
            pl.BlockSpec(pw.shape, lambda bb, i: (0, 0, 0)),
            pl.BlockSpec((1, d), fixed),
            pl.BlockSpec((1, d), fixed),
        ],
        out_specs=pl.BlockSpec((1, ts, d), tile),
        out_shape=jax.ShapeDtypeStruct((b, s, d), F32),
        scratch_shapes=[pltpu.VMEM((ts + 2 * HALO, d), F32)],
        compiler_params=_cparams(2),
        name="pool_mixer",
    )(x, x, x, sh, sc, gate, gpre, gpost, pw, pb, ps)


def _mod_rows(m, b, d):
    lat = [m[:b, k * d:(k + 1) * d].reshape(b, 1, d) for k in range(6)]
    ctx = [m[b:b + 1, k * d:(k + 1) * d].reshape(1, 1, d) for k in range(6)]
    return lat, ctx


def kernel(x, c, ctx, c_ctx, ada_w, ada_b, norm_g, w_in, w_out, na_rpb, ssm_conv_w, ssm_conv_b, ssm_a_log, ssm_dt_bias, ssm_d, ssm_norm_g, pool_w, pool_b, pool_scale, ffn_w_up, ffn_conv_w, ffn_conv_b, ffn_w_down):
    b, s, d = x.shape
    l = ctx.shape[1]
    depth = ada_w.shape[0]
    hidden = ffn_w_down.shape[1]
    rows = s // GRID_W
    assert depth == 2 and s % l == 0 and l % SSD_CHUNK == 0 and s % (NA_QROWS * GRID_W) == 0
    assert rows >= NA_KROWS and 2 * SSM_HEADS <= LANES
    assert NA_DOFF + 2 * NA_WIN_ROWS - 1 <= NA_NTILES + 1 and NA_KROWS + NA_QROWS + NA_DOFF <= NA_NTILES + 1

    rows_c = -(-(b + 1) // 8) * 8
    c_ext = jnp.zeros((rows_c, d), F32).at[:b].set(c).at[b].set(c_ctx)
    mod = _ada_call(c_ext, ada_w, ada_b)

    (sh1, sc1, g1, sh2, sc2, g2), (csh1, csc1, _, _, _, _) = _mod_rows(mod[0], b, d)
    g_pre_mix, g_post_mix, g_pre_ffn, g_post_ffn = [norm_g[0, k].reshape(1, d) for k in range(4)]
    wi = w_in[0]
    o_dt = 2 * NA_WIDTH + SSM_INNER + NA_WIDTH + SSM_CONV_DIM
    w_cat = jnp.concatenate(
        [wi[:, :o_dt], jnp.pad(wi[:, o_dt:], ((0, 0), (0, LANES - 2 * SSM_HEADS)))], axis=1).astype(BF16)
    q, z, k, v, xbc, dt_raw = _inproj_call(x, ctx, sh1, sc1, csh1, csc1, g_pre_mix, w_cat,
                                           ssm_conv_w[0], ssm_conv_b[0].reshape(1, -1))

    att = _na_call(q, k, v, _na_bias_tiles(na_rpb[0]), s)

    pad = LANES - 2 * SSM_HEADS
    prm = jnp.stack([jnp.pad(ssm_dt_bias[0].reshape(-1), (0, pad)), jnp.pad(ssm_a_log[0].reshape(-1), (0, pad))])
    dsk = jnp.repeat(ssm_d[0], SSM_HEAD_DIM, axis=1)
    y_f, y_b = _ssd_call(xbc, dt_raw, prm, dsk, s)

    x = _outproj_call(x, att, y_f, y_b, z, g1, ssm_norm_g[0].reshape(1, -1), g_post_mix,
                      w_out[0].astype(BF16), 512)
    x = _ffn_call(x, sh2, sc2, g2, g_pre_ffn, g_post_ffn, ffn_w_up[0].astype(BF16), ffn_conv_w[0],
                  ffn_conv_b[0].reshape(1, hidden), ffn_w_down[0].astype(BF16), 512, 256)

    (sh1, sc1, g1, sh2, sc2, g2), _ = _mod_rows(mod[1], b, d)
    g_pre_mix, g_post_mix, g_pre_ffn, g_post_ffn = [norm_g[1, k].reshape(1, d) for k in range(4)]
    x = _pool_call(x, sh1, sc1, g1, g_pre_mix, g_post_mix, pool_w[0].astype(BF16), pool_b[0].reshape(1, d),
                   pool_scale[0].reshape(1, d), 512)
    x = _ffn_call(x, sh2, sc2, g2, g_pre_ffn, g_post_ffn, ffn_w_up[1].astype(BF16), ffn_conv_w[1],
                  ffn_conv_b[1].reshape(1, hidden), ffn_w_down[1].astype(BF16), 512, 256)
    return x
```

```python
import functools

import numpy as np
import jax
import jax.numpy as jnp
from jax import lax
from jax.experimental import pallas as pl
from jax.experimental.pallas import tpu as pltpu

F32 = jnp.float32
BF16 = jnp.bfloat16

GRID_W = 64
NA_HEADS = 8
NA_HEAD_DIM = 64
NA_WIDTH = NA_HEADS * NA_HEAD_DIM
NA_WIN_ROWS = 8
NA_WIN_COLS = 16
SSM_HEADS = 16
SSM_HEAD_DIM = 64
SSM_INNER = SSM_HEADS * SSM_HEAD_DIM
SSM_GROUPS = 4
SSM_STATE = 128
SSD_CHUNK = 128
SSM_BC = SSM_GROUPS * SSM_STATE
SSM_CONV_DIM = SSM_INNER + 2 * SSM_BC
POOL_WINDOWS = (2, 4, 8, 16)
RMS_EPS = 1e-6

LANES = 128
HALO = 16
NA_QROWS = 8
NA_KROWS = 16
NA_STEP_BLOCKS = 2
NA_DOFF = NA_QROWS
NA_NTILES = 32
NEG = -1e30
VMEM_LIMIT = 56 * 1024 * 1024
ROW_TILE = 512
FFN_ROW_TILE = 1024
FFN_COL_TILE = 256


def _cparams(n_axes):
    return pltpu.CompilerParams(dimension_semantics=("arbitrary",) * n_axes, vmem_limit_bytes=VMEM_LIMIT)


def _rms_scale(x):
    return lax.rsqrt(jnp.mean(x * x, axis=-1, keepdims=True) + RMS_EPS)


def _norm_mod(x, g, sc, sh):
    return x * _rms_scale(x) * (g * (1.0 + sc)) + sh


def _gated_norm_add(x, y, g, gate):
    return x + y * _rms_scale(y) * (g * gate)


def _silu(x):
    return x * jax.nn.sigmoid(x)


def _dot(a, b):
    return jnp.dot(a, b, preferred_element_type=F32)


def _dot_nt(a, b):
    return lax.dot_general(a, b, (((1,), (1,)), ((), ())), preferred_element_type=F32)


def _split3(a):
    hi = a.astype(BF16)
    r = a - hi.astype(F32)
    mid = r.astype(BF16)
    lo = (r - mid.astype(F32)).astype(BF16)
    return hi, mid, lo


def _conv3_rows(u, w_rows, rows):
    n = u.shape[0]
    u_prev = pltpu.roll(u, 1, 0)[HALO:HALO + rows]
    u_next = pltpu.roll(u, n - 1, 0)[HALO:HALO + rows]
    return u_prev * w_rows[0:1] + u[HALO:HALO + rows] * w_rows[1:2] + u_next * w_rows[2:3]


def _row_spec(d, index):
    return pl.BlockSpec((None, None, 1, d), index)


def _halo_specs(tm, d, s):
    rh = tm // HALO
    return [
        pl.BlockSpec((1, HALO, d), lambda bb, i: (bb, jnp.maximum(i * rh - 1, 0), 0)),
        pl.BlockSpec((1, HALO, d), lambda bb, i: (bb, jnp.minimum((i + 1) * rh, s // HALO - 1), 0)),
    ]


def _ada_kernel(c_ref, w_ref, b_ref, o_ref):
    s = _silu(c_ref[...])
    w = w_ref[0]
    acc = jnp.zeros(o_ref.shape[1:], F32)
    s_parts = _split3(s)
    w_parts = _split3(w)
    for si, sp in enumerate(s_parts):
        for wi, wp in enumerate(w_parts):
            if si + wi <= 2:
                acc = acc + _dot(sp, wp)
    o_ref[0] = acc + b_ref[0]


def _ada_call(c_ext, ada_w, ada_b):
    depth, d, n = ada_w.shape
    r = c_ext.shape[0]
    tn = 768
    return pl.pallas_call(
        _ada_kernel,
        grid=(depth, n // tn),
        in_specs=[
            pl.BlockSpec((r, d), lambda l, j: (0, 0)),
            pl.BlockSpec((1, d, tn), lambda l, j: (l, 0, j)),
            pl.BlockSpec((1, 1, tn), lambda l, j: (l, 0, j)),
        ],
        out_specs=pl.BlockSpec((1, r, tn), lambda l, j: (l, 0, j)),
        out_shape=jax.ShapeDtypeStruct((depth, r, n), F32),
        compiler_params=_cparams(2),
        name="ada_mod",
    )(c_ext, ada_w, ada_b.reshape(depth, 1, n))


def _inproj_body(h_scr, w_ref, cw_ref, cb_ref, qz_refs, k_ref, v_ref, xbc_ref, dt_ref, tm):
    hb = h_scr[HALO:HALO + tm, :]
    if qz_refs is not None:
        q_ref, z_ref = qz_refs
        q_ref[0] = _dot(hb, w_ref[:, 0:NA_WIDTH]).astype(q_ref.dtype)
        z_ref[0] = _dot(hb, w_ref[:, NA_WIDTH:NA_WIDTH + SSM_INNER]).astype(z_ref.dtype)
    o = NA_WIDTH + SSM_INNER
    k_ref[0] = _dot(hb, w_ref[:, o:o + NA_WIDTH]).astype(k_ref.dtype)
    o += NA_WIDTH
    v_ref[0] = _dot(hb, w_ref[:, o:o + NA_WIDTH]).astype(v_ref.dtype)
    o += NA_WIDTH
    cblk = 512
    for c0 in range(0, SSM_CONV_DIM, cblk):
        u = _dot(h_scr[...], w_ref[:, o + c0:o + c0 + cblk])
        y = _conv3_rows(u, cw_ref[:, c0:c0 + cblk], tm) + cb_ref[:, c0:c0 + cblk]
        xbc_ref[0, :, c0:c0 + cblk] = _silu(y).astype(xbc_ref.dtype)
    o += SSM_CONV_DIM
    dt_ref[0] = _dot(hb, w_ref[:, o:o + LANES]).astype(dt_ref.dtype)


def _inproj_lat_kernel(x_ref, prev_ref, next_ref, sh_ref, sc_ref, g_ref, w_ref, cw_ref, cb_ref,
                       q_ref, z_ref, k_ref, v_ref, xbc_ref, dt_ref, h_scr):
    i = pl.program_id(1)
    nt = pl.num_programs(1)
    tm = x_ref.shape[1]

    def nm(xt):
        return _norm_mod(xt, g_ref[...], sc_ref[...], sh_ref[...]).astype(BF16)

    zero = jnp.zeros((), BF16)
    h_scr[0:HALO, :] = jnp.where(i > 0, nm(prev_ref[0]), zero)
    h_scr[HALO:HALO + tm, :] = nm(x_ref[0])
    h_scr[HALO + tm:, :] = jnp.where(i < nt - 1, nm(next_ref[0]), zero)
    _inproj_body(h_scr, w_ref, cw_ref, cb_ref, (q_ref, z_ref), k_ref, v_ref, xbc_ref, dt_ref, tm)


def _inproj_ctx_kernel(x_ref, sh_ref, sc_ref, g_ref, w_ref, cw_ref, cb_ref, k_in, v_in, xbc_in, dt_in,
                       k_ref, v_ref, xbc_ref, dt_ref, h_scr):
    del k_in, v_in, xbc_in, dt_in
    tm = x_ref.shape[1]
    h_scr[0:HALO, :] = jnp.zeros((HALO, h_scr.shape[1]), BF16)
    h_scr[HALO:HALO + tm, :] = _norm_mod(x_ref[0], g_ref[...], sc_ref[...], sh_ref[...]).astype(BF16)
    h_scr[HALO + tm:, :] = jnp.zeros((HALO, h_scr.shape[1]), BF16)
    _inproj_body(h_scr, w_ref, cw_ref, cb_ref, None, k_ref, v_ref, xbc_ref, dt_ref, tm)


def _inproj_call(x, ctx, mod4, norm4, w, cw, cb):
    b, s, d = x.shape
    l = ctx.shape[1]
    tm = ROW_TILE
    n = w.shape[-1]
    tile = lambda bb, i: (bb, i, 0)
    fixed = lambda bb, i: (0, 0, 0)
    weights = [
        pl.BlockSpec((None, d, n), fixed, pipeline_mode=pl.Buffered(1)),
        pl.BlockSpec((None,) + cw.shape[1:], fixed),
        pl.BlockSpec((None,) + cb.shape[1:], fixed),
    ]
    widths = (NA_WIDTH, NA_WIDTH, SSM_CONV_DIM, LANES)
    dtypes = (BF16, BF16, BF16, F32)
    q, z, k, v, xbc, dt = pl.pallas_call(
        _inproj_lat_kernel,
        grid=(b, s // tm),
        in_specs=[pl.BlockSpec((1, tm, d), tile)] + _halo_specs(tm, d, s) + [
            _row_spec(d, lambda bb, i: (0, bb, 0, 0)),
            _row_spec(d, lambda bb, i: (0, bb, 0, 1)),
            _row_spec(d, lambda bb, i: (0, 0, 0, 0)),
        ] + weights,
        out_specs=[pl.BlockSpec((1, tm, NA_WIDTH), tile), pl.BlockSpec((1, tm, SSM_INNER), tile)]
        + [pl.BlockSpec((1, tm, wd), tile) for wd in widths],
        out_shape=[jax.ShapeDtypeStruct((b, s, NA_WIDTH), BF16), jax.ShapeDtypeStruct((b, s, SSM_INNER), BF16)]
        + [jax.ShapeDtypeStruct((b, s + l, wd), dt_) for wd, dt_ in zip(widths, dtypes)],
        scratch_shapes=[pltpu.VMEM((tm + 2 * HALO, d), BF16)],
        compiler_params=_cparams(2),
        name="inproj",
    )(x, x, x, mod4, mod4, norm4, w, cw, cb)

    ctx_rows = lambda bb, i: (bb, s // l, 0)
    k, v, xbc, dt = pl.pallas_call(
        _inproj_ctx_kernel,
        grid=(b, 1),
        in_specs=[
            pl.BlockSpec((1, l, d), tile),
            _row_spec(d, lambda bb, i: (0, b, 0, 0)),
            _row_spec(d, lambda bb, i: (0, b, 0, 1)),
            _row_spec(d, lambda bb, i: (0, 0, 0, 0)),
        ] + weights + [pl.BlockSpec(memory_space=pl.ANY)] * 4,
        out_specs=[pl.BlockSpec((1, l, wd), ctx_rows) for wd in widths],
        out_shape=[jax.ShapeDtypeStruct((b, s + l, wd), dt_) for wd, dt_ in zip(widths, dtypes)],
        input_output_aliases={7: 0, 8: 1, 9: 2, 10: 3},
        scratch_shapes=[pltpu.VMEM((l + 2 * HALO, d), BF16)],
        compiler_params=_cparams(2),
        name="inproj_ctx",
    )(ctx, mod4, mod4, norm4, w, cw, cb, k, v, xbc, dt)
    return q, z, k, v, xbc, dt


def _na_bias_tiles(rpb):
    h, n_rel_r, n_rel_c = rpb.shape
    w = GRID_W
    p = jnp.pad(rpb, ((0, 0), (0, 0), (0, 2 * w - n_rel_c)))
    f = jnp.tile(p, (1, 1, w))
    t = f[..., NA_WIN_COLS - 1:NA_WIN_COLS - 1 + w * (2 * w - 1)].reshape(h, n_rel_r, w, 2 * w - 1)[..., :w]
    c = np.arange(w)[:, None]
    kc = np.arange(w)[None, :]
    cs = np.clip(c - NA_WIN_COLS // 2, 0, w - NA_WIN_COLS)
    col_ok = jnp.asarray((kc >= cs) & (kc < cs + NA_WIN_COLS))
    t = jnp.where(col_ok[None, None], t, NEG)
    n_after = NA_NTILES + 1 - NA_DOFF - n_rel_r
    tz = jnp.concatenate(
        [jnp.full((h, NA_DOFF, w, w), NEG, F32), t, jnp.full((h, n_after, w, w), NEG, F32)], axis=1)
    return jnp.concatenate([tz[:, :-1], tz[:, 1:]], axis=-1)


def _na_kernel(q_ref, k_ref, v_ref, pt_ref, o_ref, *, rows, n_lat):
    nk = NA_KROWS * GRID_W
    nq = NA_QROWS * GRID_W
    w = GRID_W
    kc = k_ref[0, n_lat:, :]
    vc = v_ref[0, n_lat:, :]
    lane = lax.broadcasted_iota(jnp.int32, (1, LANES), 1)
    left = lane < w
    one = jnp.ones((), v_ref.dtype)
    scale = jnp.asarray(NA_HEAD_DIM ** -0.5, q_ref.dtype)
    for sub in range(NA_STEP_BLOCKS):
        r0 = (pl.program_id(2) * NA_STEP_BLOCKS + sub) * NA_QROWS
        k0 = jnp.clip(r0 - NA_WIN_ROWS // 2, 0, rows - NA_KROWS)
        kstart = pl.multiple_of(k0 * GRID_W, GRID_W)
        q = q_ref[0, sub * nq:(sub + 1) * nq, :]
        kw = k_ref[0, pl.ds(kstart, nk), :]
        vw = v_ref[0, pl.ds(kstart, nk), :]
        out = jnp.zeros((nq, LANES), F32)
        for hh in range(LANES // NA_HEAD_DIM):
            sel = jnp.logical_and(lane >= hh * NA_HEAD_DIM, lane < (hh + 1) * NA_HEAD_DIM)
            qm = jnp.where(sel, q, jnp.zeros_like(q)) * scale
            s_raw = _dot_nt(qm, kw)
            row_blocks = []
            for i in range(NA_QROWS):
                lo = jnp.clip(r0 + i - NA_WIN_ROWS // 2, 0, rows - NA_WIN_ROWS) - k0
                blocks = []
                for jj in range(NA_KROWS // 2):
                    t_idx = (k0 - r0) + 2 * jj - i + (NA_WIN_ROWS - 1) + NA_DOFF
                    ok0 = jnp.logical_and(2 * jj >= lo, 2 * jj < lo + NA_WIN_ROWS)
                    ok1 = jnp.logical_and(2 * jj + 1 >= lo, 2 * jj + 1 < lo + NA_WIN_ROWS)
                    mvec = jnp.where(left, jnp.where(ok0, 0.0, NEG), jnp.where(ok1, 0.0, NEG))
                    blk = s_raw[i * w:(i + 1) * w, jj * 2 * w:(jj + 1) * 2 * w]
                    blocks.append(blk + pt_ref[hh, t_idx] + mvec)
                row_blocks.append(jnp.concatenate(blocks, axis=1))
            s = jnp.concatenate(row_blocks, axis=0)
            sc = _dot_nt(qm, kc)
            m = jnp.maximum(jnp.max(s, axis=-1, keepdims=True), jnp.max(sc, axis=-1, keepdims=True))
            p = jnp.exp(s - m).astype(BF16)
            pc = jnp.exp(sc - m).astype(BF16)
            o = _dot(p, jnp.where(sel, vw, one)) + _dot(pc, jnp.where(sel, vc, one))
            den = pltpu.roll(o, NA_HEAD_DIM, 1)
            out = jnp.where(sel, o / den, out)
        o_ref[0, sub * nq:(sub + 1) * nq, :] = out.astype(o_ref.dtype)


def _na_call(q, k, v, pt, n_lat):
    b, s, _ = q.shape
    t = k.shape[1]
    rows = s // GRID_W
    nq = NA_STEP_BLOCKS * NA_QROWS * GRID_W
    hp = LANES // NA_HEAD_DIM
    return pl.pallas_call(
        functools.partial(_na_kernel, rows=rows, n_lat=n_lat),
        grid=(NA_WIDTH // LANES, b, s // nq),
        in_specs=[
            pl.BlockSpec((1, nq, LANES), lambda p, bb, rb: (bb, rb, p)),
            pl.BlockSpec((1, t, LANES), lambda p, bb, rb: (bb, 0, p)),
            pl.BlockSpec((1, t, LANES), lambda p, bb, rb: (bb, 0, p)),
            pl.BlockSpec((hp,) + pt.shape[1:], lambda p, bb, rb: (p, 0, 0, 0)),
        ],
        out_specs=pl.BlockSpec((1, nq, LANES), lambda p, bb, rb: (bb, rb, p)),
        out_shape=jax.ShapeDtypeStruct((b, s, NA_WIDTH), BF16),
        compiler_params=_cparams(3),
        name="nbr_attention",
    )(q, k, v, pt)


def _ssd_decays(dt_ref, prm_ref, *, reverse):
    q = SSD_CHUNK
    dt = jax.nn.softplus(dt_ref[0] + prm_ref[0:1, :])
    a = -dt * jnp.exp(prm_ref[1:2, :])
    ri = lax.broadcasted_iota(jnp.int32, (q, q), 0)
    ci = lax.broadcasted_iota(jnp.int32, (q, q), 1)
    tri = (ci >= ri) if reverse else (ci <= ri)
    ones = jnp.where(tri, 1.0, 0.0).astype(BF16)
    cs = jnp.zeros((q, LANES), F32)
    for part in _split3(a):
        cs = cs + _dot(ones, part)
    tot_row = cs[q - 1:q, :] if not reverse else cs[0:1, :]
    cs_t = cs.T
    dt_t = dt.T
    e_end_t = jnp.exp(tot_row.T - cs_t) * dt_t
    e_start = jnp.exp(cs)
    dec = jnp.exp(tot_row)
    return tri, cs, cs_t, dt_t, e_end_t, e_start, dec


def _ssd_direction(xbc_ref, decays, dsk_ref, state_ref, y_ref, *, reverse, emit_y):
    tri, cs, cs_t, dt_t, e_end_t, e_start, dec = decays
    lane_off = SSM_HEADS if reverse else 0
    lane = lax.broadcasted_iota(jnp.int32, (1, LANES), 1)
    lo = lane < SSM_HEAD_DIM
    for g in range(SSM_GROUPS):
        bm = xbc_ref[0, :, SSM_INNER + g * SSM_STATE:SSM_INNER + (g + 1) * SSM_STATE]
        cm = xbc_ref[0, :, SSM_INNER + SSM_BC + g * SSM_STATE:SSM_INNER + SSM_BC + (g + 1) * SSM_STATE]
        bt = bm.astype(F32).T
        if emit_y:
            cb = _dot_nt(cm, bm)
            cmf = cm.astype(F32)
        heads_per_group = SSM_HEADS // SSM_GROUPS
        for pp in range(heads_per_group // 2):
            pair = g * (heads_per_group // 2) + pp
            xs = xbc_ref[0, :, pair * LANES:(pair + 1) * LANES]
            xs_lo = jnp.where(lo, xs, jnp.zeros_like(xs))
            xs_hi = jnp.where(lo, jnp.zeros_like(xs), xs)
            rhs_x = jnp.concatenate([xs_lo, xs_hi], axis=0)
            st = state_ref[pair]
            h0 = lane_off + 2 * pair
            btw = [bt * e_end_t[h0 + u:h0 + u + 1, :] for u in range(2)]
            upd = _dot(jnp.concatenate(btw, axis=1).astype(BF16), rhs_x)
            if emit_y:
                stb = st.astype(BF16)
                st_lo = jnp.where(lo, stb, jnp.zeros_like(stb))
                st_hi = jnp.where(lo, jnp.zeros_like(stb), stb)
                lhs = []
                for u in range(2):
                    hcol = cs[:, h0 + u:h0 + u + 1]
                    hrow = cs_t[h0 + u:h0 + u + 1, :]
                    seg = jnp.exp(jnp.where(tri, hcol - hrow, NEG))
                    lhs.append(cb * seg * dt_t[h0 + u:h0 + u + 1, :])
                for u in range(2):
                    lhs.append(cmf * e_start[:, h0 + u:h0 + u + 1])
                y = _dot(jnp.concatenate(lhs, axis=1).astype(BF16),
                         jnp.concatenate([rhs_x, st_lo, st_hi], axis=0))
                y = y + dsk_ref[:, pair * LANES:(pair + 1) * LANES] * xs.astype(F32)
                y_ref[0, :, pair * LANES:(pair + 1) * LANES] = y.astype(y_ref.dtype)
            dpair = jnp.where(lo, dec[:, h0:h0 + 1], dec[:, h0 + 1:h0 + 2])
            state_ref[pair] = st * dpair + upd


def _ssd_kernel(xf_ref, dtf_ref, xb_ref, dtb_ref, prm_ref, dsk_ref, yf_ref, yb_ref, sf_ref, sb_ref, *, n_ctx):
    s = pl.program_id(1)

    @pl.when(s == 0)
    def _():
        sf_ref[...] = jnp.zeros_like(sf_ref)
        sb_ref[...] = jnp.zeros_like(sb_ref)

    def run(emit_y):
        dec_f = _ssd_decays(dtf_ref, prm_ref, reverse=False)
        dec_b = _ssd_decays(dtb_ref, prm_ref, reverse=True)
        _ssd_direction(xf_ref, dec_f, dsk_ref.at[0:1], sf_ref, yf_ref, reverse=False, emit_y=emit_y)
        _ssd_direction(xb_ref, dec_b, dsk_ref.at[1:2], sb_ref, yb_ref, reverse=True, emit_y=emit_y)

    @pl.when(s < n_ctx)
    def _():
        run(False)

    @pl.when(s >= n_ctx)
    def _():
        run(True)


def _ssd_call(xbc, dt, prm, dsk, n_lat):
    b, t, c = xbc.shape
    q = SSD_CHUNK
    nc = t // q
    nl = n_lat // q
    n_ctx = nc - nl
    fwd = lambda bb, s: (bb, jnp.where(s < n_ctx, nl + s, s - n_ctx), 0)
    bwd = lambda bb, s: (bb, nc - 1 - s, 0)
    yf = lambda bb, s: (bb, jnp.maximum(s - n_ctx, 0), 0)
    yb = lambda bb, s: (bb, jnp.minimum(nc - 1 - s, nl - 1), 0)
    fixed = lambda bb, s: (0, 0)
    return pl.pallas_call(
        functools.partial(_ssd_kernel, n_ctx=n_ctx),
        grid=(b, nc),
        in_specs=[
            pl.BlockSpec((1, q, c), fwd),
            pl.BlockSpec((1, q, LANES), fwd),
            pl.BlockSpec((1, q, c), bwd),
            pl.BlockSpec((1, q, LANES), bwd),
            pl.BlockSpec((2, LANES), fixed),
            pl.BlockSpec((2, SSM_INNER), fixed),
        ],
        out_specs=[
            pl.BlockSpec((1, q, SSM_INNER), yf),
            pl.BlockSpec((1, q, SSM_INNER), yb),
        ],
        out_shape=[jax.ShapeDtypeStruct((b, n_lat, SSM_INNER), BF16)] * 2,
        scratch_shapes=[pltpu.VMEM((SSM_HEADS // 2, SSM_STATE, LANES), F32)] * 2,
        compiler_params=_cparams(2),
        name="ssd_scan",
    )(xbc, dt, xbc, dt, prm, dsk)


def _outproj_kernel(x_ref, att_ref, yf_ref, yb_ref, z_ref, gate_ref, ng_ref, gpost_ref, w_ref, o_ref):
    y = yf_ref[0].astype(F32) + yb_ref[0].astype(F32)
    yz = y * _silu(z_ref[0].astype(F32))
    gw = SSM_INNER // SSM_GROUPS
    parts = []
    for g in range(SSM_GROUPS):
        seg = yz[:, g * gw:(g + 1) * gw]
        parts.append(seg * _rms_scale(seg))
    ssm = (jnp.concatenate(parts, axis=-1) * ng_ref[...]).astype(BF16)
    o = _dot(att_ref[0], w_ref[0:NA_WIDTH, :]) + _dot(ssm, w_ref[NA_WIDTH:, :])
    o_ref[0] = _gated_norm_add(x_ref[0], o, gpost_ref[...], gate_ref[...])


def _outproj_call(x, att, yf, yb, z, mod4, norm4, ng, w):
    b, s, d = x.shape
    tm = ROW_TILE
    tile = lambda bb, i: (bb, i, 0)
    return pl.pallas_call(
        _outproj_kernel,
        grid=(b, s // tm),
        in_specs=[
            pl.BlockSpec((1, tm, d), tile),
            pl.BlockSpec((1, tm, NA_WIDTH), tile),
            pl.BlockSpec((1, tm, SSM_INNER), tile),
            pl.BlockSpec((1, tm, SSM_INNER), tile),
            pl.BlockSpec((1, tm, SSM_INNER), tile),
            _row_spec(d, lambda bb, i: (0, bb, 0, 2)),
            pl.BlockSpec((1, SSM_INNER), lambda bb, i: (0, 0)),
            _row_spec(d, lambda bb, i: (0, 1, 0, 0)),
            pl.BlockSpec((None,) + w.shape[1:], lambda bb, i: (0, 0, 0), pipeline_mode=pl.Buffered(1)),
        ],
        out_specs=pl.BlockSpec((1, tm, d), tile),
        out_shape=jax.ShapeDtypeStruct((b, s, d), F32),
        compiler_params=_cparams(2),
        name="outproj",
    )(x, att, yf, yb, z, mod4, ng, norm4, w)


def _ffn_kernel(x_ref, prev_ref, next_ref, sh_ref, sc_ref, gate_ref, gpre_ref, gpost_ref,
                wup_ref, cw_ref, cb_ref, wdn_ref, o_ref, h_scr, *, tn):
    i = pl.program_id(1)
    nt = pl.num_programs(1)
    tm = x_ref.shape[1]
    hidden = wdn_ref.shape[0]
    x = x_ref[0]

    def nm(xt):
        return _norm_mod(xt, gpre_ref[...], sc_ref[...], sh_ref[...]).astype(BF16)

    zero = jnp.zeros((), BF16)
    h_scr[0:HALO, :] = jnp.where(i > 0, nm(prev_ref[0]), zero)
    h_scr[HALO:HALO + tm, :] = nm(x)
    h_scr[HALO + tm:, :] = jnp.where(i < nt - 1, nm(next_ref[0]), zero)
    acc = jnp.zeros((tm, x.shape[1]), F32)
    for j in range(hidden // tn):
        u = _dot(h_scr[...], wup_ref[:, j * tn:(j + 1) * tn])
        v = _dot(h_scr[HALO:HALO + tm, :], wup_ref[:, hidden + j * tn:hidden + (j + 1) * tn])
        cv = _conv3_rows(u, cw_ref[:, j * tn:(j + 1) * tn], tm) + cb_ref[:, j * tn:(j + 1) * tn]
        gl = 0.5 * cv * (1.0 + lax.erf(cv * 0.7071067811865476)) * v
        acc = acc + _dot(gl.astype(BF16), wdn_ref[j * tn:(j + 1) * tn, :])
    o_ref[0] = _gated_norm_add(x, acc, gpost_ref[...], gate_ref[...])


def _ffn_call(x, layer, mod4, norm4, wup, cw, cb, wdn):
    b, s, d = x.shape
    hidden = wdn.shape[1]
    tm = FFN_ROW_TILE
    tile = lambda bb, i: (bb, i, 0)
    at_layer = lambda bb, i: (layer, 0, 0)
    return pl.pallas_call(
        functools.partial(_ffn_kernel, tn=FFN_COL_TILE),
        grid=(b, s // tm),
        in_specs=[pl.BlockSpec((1, tm, d), tile)] + _halo_specs(tm, d, s) + [
            _row_spec(d, lambda bb, i: (layer, bb, 0, 3)),
            _row_spec(d, lambda bb, i: (layer, bb, 0, 4)),
            _row_spec(d, lambda bb, i: (layer, bb, 0, 5)),
            _row_spec(d, lambda bb, i: (layer, 2, 0, 0)),
            _row_spec(d, lambda bb, i: (layer, 3, 0, 0)),
            pl.BlockSpec((None, d, 2 * hidden), at_layer, pipeline_mode=pl.Buffered(1)),
            pl.BlockSpec((None,) + cw.shape[1:], at_layer),
            pl.BlockSpec((None, 1, hidden), at_layer),
            pl.BlockSpec((None, hidden, d), at_layer, pipeline_mode=pl.Buffered(1)),
        ],
        out_specs=pl.BlockSpec((1, tm, d), tile),
        out_shape=jax.ShapeDtypeStruct((b, s, d), F32),
        scratch_shapes=[pltpu.VMEM((tm + 2 * HALO, d), BF16)],
        compiler_params=_cparams(2),
        name="conv_ffn",
    )(x, x, x, mod4, mod4, mod4, norm4, norm4, wup, cw, cb, wdn)


def _window_sum(h_ext, w, rows):
    n = h_ext.shape[0]
    acc = h_ext
    span = 1
    while span < w:
        acc = acc + pltpu.roll(acc, span, 0)
        span *= 2
    shift = w // 2 - 1
    if shift:
        acc = pltpu.roll(acc, n - shift, 0)
    return acc[HALO:HALO + rows]


def _pool_kernel(x_ref, prev_ref, next_ref, sh_ref, sc_ref, gate_ref, gpre_ref, gpost_ref,
                 pw_ref, pb_ref, ps_ref, o_ref, h_scr, *, seq):
    i = pl.program_id(1)
    nt = pl.num_programs(1)
    ts = x_ref.shape[1]
    x = x_ref[0]
    gwidth = pw_ref.shape[1]

    def nm(xt):
        return _norm_mod(xt, gpre_ref[...], sc_ref[...], sh_ref[...])

    h_scr[0:HALO, :] = jnp.where(i > 0, nm(prev_ref[0]), 0.0)
    h_scr[HALO:HALO + ts, :] = nm(x)
    h_scr[HALO + ts:, :] = jnp.where(i < nt - 1, nm(next_ref[0]), 0.0)
    t = i * ts + lax.broadcasted_iota(jnp.int32, (ts, 1), 0)
    ys = []
    for gi, w in enumerate(POOL_WINDOWS):
        cols = slice(gi * gwidth, (gi + 1) * gwidth)
        h_ext = h_scr[:, cols]
        cnt = (jnp.minimum(t + w // 2, seq) - jnp.maximum(t - w // 2, 0)).astype(F32)
        pooled = _window_sum(h_ext, w, ts) / cnt - h_ext[HALO:HALO + ts]
        ys.append(_dot(pooled.astype(BF16), pw_ref[gi]) + pb_ref[:, cols])
    y = jnp.concatenate(ys, axis=-1) * ps_ref[...]
    o_ref[0] = _gated_norm_add(x, y, gpost_ref[...], gate_ref[...])


def _pool_call(x, layer, mod4, norm4, pw, pb, ps):
    b, s, d = x.shape
    ts = ROW_TILE
    tile = lambda bb, i: (bb, i, 0)
    fixed = lambda bb, i: (0, 0)
    return pl.pallas_call(
        functools.partial(_pool_kernel, seq=s),
        grid=(b, s // ts),
        in_specs=[pl.BlockSpec((1, ts, d), tile)] + _halo_specs(ts, d, s) + [
            _row_spec(d, lambda bb, i: (layer, bb, 0, 0)),
            _row_spec(d, lambda bb, i: (layer, bb, 0, 1)),
            _row_spec(d, lambda bb, i: (layer, bb, 0, 2)),
            _row_spec(d, lambda bb, i: (layer, 0, 0, 0)),
            _row_spec(d, lambda bb, i: (layer, 1, 0, 0)),
            pl.BlockSpec(pw.shape, lambda bb, i: (0, 0, 0)),
            pl.BlockSpec((1, d), fixed),
            pl.BlockSpec((1, d), fixed),
        ],
        out_specs=pl.BlockSpec((1, ts, d), tile),
        out_shape=jax.ShapeDtypeStruct((b, s, d), F32),
        scratch_shapes=[pltpu.VMEM((ts + 2 * HALO, d), F32)],
        compiler_params=_cparams(2),
        name="pool_mixer",
    )(x, x, x, mod4, mod4, mod4, norm4, norm4, pw, pb, ps)


def kernel(x, c, ctx, c_ctx, ada_w, ada_b, norm_g, w_in, w_out, na_rpb, ssm_conv_w, ssm_conv_b, ssm_a_log, ssm_dt_bias, ssm_d, ssm_norm_g, pool_w, pool_b, pool_scale, ffn_w_up, ffn_conv_w, ffn_conv_b, ffn_w_down):
    b, s, d = x.shape
    l = ctx.shape[1]
    depth = ada_w.shape[0]
    hidden = ffn_w_down.shape[1]
    rows = s // GRID_W
    assert depth == 2 and l % SSD_CHUNK == 0 and s % l == 0 and s % (NA_STEP_BLOCKS * NA_QROWS * GRID_W) == 0
    assert s % FFN_ROW_TILE == 0 and s % ROW_TILE == 0 and hidden % FFN_COL_TILE == 0
    assert rows >= NA_KROWS and 2 * SSM_HEADS <= LANES
    assert NA_DOFF + 2 * NA_WIN_ROWS - 1 <= NA_NTILES + 1 and NA_KROWS + NA_QROWS + NA_DOFF <= NA_NTILES + 1

    rows_c = -(-(b + 1) // 8) * 8
    c_ext = jnp.zeros((rows_c, d), F32).at[:b].set(c).at[b].set(c_ctx)
    mod4 = _ada_call(c_ext, ada_w, ada_b).reshape(depth, rows_c, 1, 6 * d)
    norm4 = norm_g.reshape(depth, 4, 1, d)
    wup = ffn_w_up.astype(BF16)
    wdn = ffn_w_down.astype(BF16)
    ffn_cb = ffn_conv_b.reshape(depth, 1, hidden)

    w_cat = jnp.pad(w_in, ((0, 0), (0, 0), (0, LANES - 2 * SSM_HEADS))).astype(BF16)
    q, z, k, v, xbc, dt_raw = _inproj_call(x, ctx, mod4, norm4, w_cat, ssm_conv_w,
                                           ssm_conv_b.reshape(1, 1, SSM_CONV_DIM))
    att = _na_call(q, k, v, _na_bias_tiles(na_rpb[0]), s)
    pad = LANES - 2 * SSM_HEADS
    prm = jnp.stack([jnp.pad(ssm_dt_bias[0].reshape(-1), (0, pad)), jnp.pad(ssm_a_log[0].reshape(-1), (0, pad))])
    dsk = jnp.repeat(ssm_d[0], SSM_HEAD_DIM, axis=1)
    y_f, y_b = _ssd_call(xbc, dt_raw, prm, dsk, s)
    x = _outproj_call(x, att, y_f, y_b, z, mod4, norm4, ssm_norm_g.reshape(1, SSM_INNER), w_out.astype(BF16))
    x = _ffn_call(x, 0, mod4, norm4, wup, ffn_conv_w, ffn_cb, wdn)

    x = _pool_call(x, 1, mod4, norm4, pool_w[0].astype(BF16), pool_b.reshape(1, d), pool_scale.reshape(1, d))
    x = _ffn_call(x, 1, mod4, norm4, wup, ffn_conv_w, ffn_cb, wdn)
    return x
```

```python
import functools

import numpy as np
import jax
import jax.numpy as jnp
from jax import lax
from jax.experimental import pallas as pl
from jax.experimental.pallas import tpu as pltpu

F32 = jnp.float32
BF16 = jnp.bfloat16

GRID_W = 64
NA_HEADS = 8
NA_HEAD_DIM = 64
NA_WIDTH = NA_HEADS * NA_HEAD_DIM
NA_WIN_ROWS = 8
NA_WIN_COLS = 16
SSM_HEADS = 16
SSM_HEAD_DIM = 64
SSM_INNER = SSM_HEADS * SSM_HEAD_DIM
SSM_GROUPS = 4
SSM_STATE = 128
SSD_CHUNK = 128
SSD_STEP_CHUNKS = 2
SSM_BC = SSM_GROUPS * SSM_STATE
SSM_CONV_DIM = SSM_INNER + 2 * SSM_BC
POOL_WINDOWS = (2, 4, 8, 16)
RMS_EPS = 1e-6

LANES = 128
HALO = 16
NA_QROWS = 4
NA_KROWS = 12
NA_STEP_BLOCKS = 4
NA_DOFF = NA_QROWS
NA_NTILES = 32
NEG = -1e30
VMEM_LIMIT = 56 * 1024 * 1024
ROW_TILE = 512
FFN_ROW_TILE = 1024
FFN_COL_TILE = 256


def _cparams(n_axes):
    return pltpu.CompilerParams(dimension_semantics=("arbitrary",) * n_axes, vmem_limit_bytes=VMEM_LIMIT)


def _rms_scale(x):
    return lax.rsqrt(jnp.mean(x * x, axis=-1, keepdims=True) + RMS_EPS)


def _norm_mod(x, g, sc, sh):
    return x * _rms_scale(x) * (g * (1.0 + sc)) + sh


def _gated_norm_add(x, y, g, gate):
    return x + y * _rms_scale(y) * (g * gate)


def _silu(x):
    return x * jax.nn.sigmoid(x)


def _dot(a, b):
    return jnp.dot(a, b, preferred_element_type=F32)


def _dot_nt(a, b):
    return lax.dot_general(a, b, (((1,), (1,)), ((), ())), preferred_element_type=F32)


def _split3(a):
    hi = a.astype(BF16)
    r = a - hi.astype(F32)
    mid = r.astype(BF16)
    lo = (r - mid.astype(F32)).astype(BF16)
    return hi, mid, lo


def _conv3_rows(u, w_rows, rows):
    n = u.shape[0]
    u_prev = pltpu.roll(u, 1, 0)[HALO:HALO + rows]
    u_next = pltpu.roll(u, n - 1, 0)[HALO:HALO + rows]
    return u_prev * w_rows[0:1] + u[HALO:HALO + rows] * w_rows[1:2] + u_next * w_rows[2:3]


def _row_spec(d, index):
    return pl.BlockSpec((None, None, 1, d), index)


def _halo_specs(tm, d, s):
    rh = tm // HALO
    return [
        pl.BlockSpec((1, HALO, d), lambda bb, i: (bb, jnp.maximum(i * rh - 1, 0), 0)),
        pl.BlockSpec((1, HALO, d), lambda bb, i: (bb, jnp.minimum((i + 1) * rh, s // HALO - 1), 0)),
    ]


def _ada_kernel(c_ref, w_ref, b_ref, o_ref):
    s = _silu(c_ref[...])
    w = w_ref[0]
    acc = jnp.zeros(o_ref.shape[1:], F32)
    s_parts = _split3(s)
    w_parts = _split3(w)
    for si, sp in enumerate(s_parts):
        for wi, wp in enumerate(w_parts):
            if si + wi <= 2:
                acc = acc + _dot(sp, wp)
    o_ref[0] = acc + b_ref[0]


def _ada_call(c_ext, ada_w, ada_b):
    depth, d, n = ada_w.shape
    r = c_ext.shape[0]
    tn = 768
    return pl.pallas_call(
        _ada_kernel,
        grid=(depth, n // tn),
        in_specs=[
            pl.BlockSpec((r, d), lambda l, j: (0, 0)),
            pl.BlockSpec((1, d, tn), lambda l, j: (l, 0, j)),
            pl.BlockSpec((1, 1, tn), lambda l, j: (l, 0, j)),
        ],
        out_specs=pl.BlockSpec((1, r, tn), lambda l, j: (l, 0, j)),
        out_shape=jax.ShapeDtypeStruct((depth, r, n), F32),
        compiler_params=_cparams(2),
        name="ada_mod",
    )(c_ext, ada_w, ada_b.reshape(depth, 1, n))


def _inproj_body(h_scr, w_ref, cw_ref, cb_ref, qz_refs, k_ref, v_ref, xbc_ref, dt_ref, tm):
    hb = h_scr[HALO:HALO + tm, :]
    if qz_refs is not None:
        q_ref, z_ref = qz_refs
        q_ref[0] = _dot(hb, w_ref[:, 0:NA_WIDTH]).astype(q_ref.dtype)
        z_ref[0] = _dot(hb, w_ref[:, NA_WIDTH:NA_WIDTH + SSM_INNER]).astype(z_ref.dtype)
    o = NA_WIDTH + SSM_INNER
    k_ref[0] = _dot(hb, w_ref[:, o:o + NA_WIDTH]).astype(k_ref.dtype)
    o += NA_WIDTH
    v_ref[0] = _dot(hb, w_ref[:, o:o + NA_WIDTH]).astype(v_ref.dtype)
    o += NA_WIDTH
    cblk = 512
    for c0 in range(0, SSM_CONV_DIM, cblk):
        u = _dot(h_scr[...], w_ref[:, o + c0:o + c0 + cblk])
        y = _conv3_rows(u, cw_ref[:, c0:c0 + cblk], tm) + cb_ref[:, c0:c0 + cblk]
        xbc_ref[0, :, c0:c0 + cblk] = _silu(y).astype(xbc_ref.dtype)
    o += SSM_CONV_DIM
    dt_ref[0] = _dot(hb, w_ref[:, o:o + LANES]).astype(dt_ref.dtype)


def _inproj_lat_kernel(x_ref, prev_ref, next_ref, sh_ref, sc_ref, g_ref, w_ref, cw_ref, cb_ref,
                       q_ref, z_ref, k_ref, v_ref, xbc_ref, dt_ref, h_scr):
    i = pl.program_id(1)
    nt = pl.num_programs(1)
    tm = x_ref.shape[1]

    def nm(xt):
        return _norm_mod(xt, g_ref[...], sc_ref[...], sh_ref[...]).astype(BF16)

    zero = jnp.zeros((), BF16)
    h_scr[0:HALO, :] = jnp.where(i > 0, nm(prev_ref[0]), zero)
    h_scr[HALO:HALO + tm, :] = nm(x_ref[0])
    h_scr[HALO + tm:, :] = jnp.where(i < nt - 1, nm(next_ref[0]), zero)
    _inproj_body(h_scr, w_ref, cw_ref, cb_ref, (q_ref, z_ref), k_ref, v_ref, xbc_ref, dt_ref, tm)


def _inproj_ctx_kernel(x_ref, sh_ref, sc_ref, g_ref, w_ref, cw_ref, cb_ref, k_in, v_in, xbc_in, dt_in,
                       k_ref, v_ref, xbc_ref, dt_ref, h_scr):
    del k_in, v_in, xbc_in, dt_in
    tm = x_ref.shape[1]
    h_scr[0:HALO, :] = jnp.zeros((HALO, h_scr.shape[1]), BF16)
    h_scr[HALO:HALO + tm, :] = _norm_mod(x_ref[0], g_ref[...], sc_ref[...], sh_ref[...]).astype(BF16)
    h_scr[HALO + tm:, :] = jnp.zeros((HALO, h_scr.shape[1]), BF16)
    _inproj_body(h_scr, w_ref, cw_ref, cb_ref, None, k_ref, v_ref, xbc_ref, dt_ref, tm)


def _inproj_call(x, ctx, mod4, norm4, w, cw, cb):
    b, s, d = x.shape
    l = ctx.shape[1]
    tm = ROW_TILE
    n = w.shape[-1]
    tile = lambda bb, i: (bb, i, 0)
    fixed = lambda bb, i: (0, 0, 0)
    weights = [
        pl.BlockSpec((None, d, n), fixed, pipeline_mode=pl.Buffered(1)),
        pl.BlockSpec((None,) + cw.shape[1:], fixed),
        pl.BlockSpec((None,) + cb.shape[1:], fixed),
    ]
    widths = (NA_WIDTH, NA_WIDTH, SSM_CONV_DIM, LANES)
    dtypes = (BF16, BF16, BF16, F32)
    q, z, k, v, xbc, dt = pl.pallas_call(
        _inproj_lat_kernel,
        grid=(b, s // tm),
        in_specs=[pl.BlockSpec((1, tm, d), tile)] + _halo_specs(tm, d, s) + [
            _row_spec(d, lambda bb, i: (0, bb, 0, 0)),
            _row_spec(d, lambda bb, i: (0, bb, 0, 1)),
            _row_spec(d, lambda bb, i: (0, 0, 0, 0)),
        ] + weights,
        out_specs=[pl.BlockSpec((1, tm, NA_WIDTH), tile), pl.BlockSpec((1, tm, SSM_INNER), tile)]
        + [pl.BlockSpec((1, tm, wd), tile) for wd in widths],
        out_shape=[jax.ShapeDtypeStruct((b, s, NA_WIDTH), BF16), jax.ShapeDtypeStruct((b, s, SSM_INNER), BF16)]
        + [jax.ShapeDtypeStruct((b, s + l, wd), dt_) for wd, dt_ in zip(widths, dtypes)],
        scratch_shapes=[pltpu.VMEM((tm + 2 * HALO, d), BF16)],
        compiler_params=_cparams(2),
        name="inproj",
    )(x, x, x, mod4, mod4, norm4, w, cw, cb)

    ctx_rows = lambda bb, i: (bb, s // l, 0)
    k, v, xbc, dt = pl.pallas_call(
        _inproj_ctx_kernel,
        grid=(b, 1),
        in_specs=[
            pl.BlockSpec((1, l, d), tile),
            _row_spec(d, lambda bb, i: (0, b, 0, 0)),
            _row_spec(d, lambda bb, i: (0, b, 0, 1)),
            _row_spec(d, lambda bb, i: (0, 0, 0, 0)),
        ] + weights + [pl.BlockSpec(memory_space=pl.ANY)] * 4,
        out_specs=[pl.BlockSpec((1, l, wd), ctx_rows) for wd in widths],
        out_shape=[jax.ShapeDtypeStruct((b, s + l, wd), dt_) for wd, dt_ in zip(widths, dtypes)],
        input_output_aliases={7: 0, 8: 1, 9: 2, 10: 3},
        scratch_shapes=[pltpu.VMEM((l + 2 * HALO, d), BF16)],
        compiler_params=_cparams(2),
        name="inproj_ctx",
    )(ctx, mod4, mod4, norm4, w, cw, cb, k, v, xbc, dt)
    return q, z, k, v, xbc, dt


def _na_bias_tiles(rpb):
    h, n_rel_r, n_rel_c = rpb.shape
    w = GRID_W
    c = np.arange(w)[:, None]
    kc = np.arange(w)[None, :]
    cs = np.clip(c - NA_WIN_COLS // 2, 0, w - NA_WIN_COLS)
    col_ok = (kc >= cs) & (kc < cs + NA_WIN_COLS)
    pick = ((kc - c + NA_WIN_COLS - 1)[None] == np.arange(n_rel_c)[:, None, None]) & col_ok[None]
    t = jnp.einsum("hrd,dck->hrck", rpb, jnp.asarray(pick, F32), precision=lax.Precision.HIGHEST)
    t = jnp.where(jnp.asarray(col_ok)[None, None], t, NEG)
    n_after = NA_NTILES + 1 - NA_DOFF - n_rel_r
    tz = jnp.concatenate(
        [jnp.full((h, NA_DOFF, w, w), NEG, F32), t, jnp.full((h, n_after, w, w), NEG, F32)], axis=1)
    return jnp.concatenate([tz[:, :-1], tz[:, 1:]], axis=-1)


def _na_kernel(q_ref, k_ref, v_ref, pt_ref, o_ref, *, rows, n_lat):
    nk = NA_KROWS * GRID_W
    nq = NA_QROWS * GRID_W
    w = GRID_W
    kc = k_ref[0, n_lat:, :]
    vc = v_ref[0, n_lat:, :]
    lane = lax.broadcasted_iota(jnp.int32, (1, LANES), 1)
    left = lane < w
    one = jnp.ones((), v_ref.dtype)
    scale = jnp.asarray(NA_HEAD_DIM ** -0.5, q_ref.dtype)
    for sub in range(NA_STEP_BLOCKS):
        r0 = (pl.program_id(2) * NA_STEP_BLOCKS + sub) * NA_QROWS
        k0 = jnp.clip(r0 - NA_WIN_ROWS // 2, 0, rows - NA_KROWS)
        kstart = pl.multiple_of(k0 * GRID_W, GRID_W)
        q = q_ref[0, sub * nq:(sub + 1) * nq, :]
        kw = k_ref[0, pl.ds(kstart, nk), :]
        vw = v_ref[0, pl.ds(kstart, nk), :]
        out = jnp.zeros((nq, LANES), F32)
        for hh in range(LANES // NA_HEAD_DIM):
            sel = jnp.logical_and(lane >= hh * NA_HEAD_DIM, lane < (hh + 1) * NA_HEAD_DIM)
            qm = jnp.where(sel, q, jnp.zeros_like(q)) * scale
            s_raw = _dot_nt(qm, kw)
            row_blocks = []
            for i in range(NA_QROWS):
                lo = jnp.clip(r0 + i - NA_WIN_ROWS // 2, 0, rows - NA_WIN_ROWS) - k0
                blocks = []
                for jj in range(NA_KROWS // 2):
                    t_idx = (k0 - r0) + 2 * jj - i + (NA_WIN_ROWS - 1) + NA_DOFF
                    ok0 = jnp.logical_and(2 * jj >= lo, 2 * jj < lo + NA_WIN_ROWS)
                    ok1 = jnp.logical_and(2 * jj + 1 >= lo, 2 * jj + 1 < lo + NA_WIN_ROWS)
                    mvec = jnp.where(left, jnp.where(ok0, 0.0, NEG), jnp.where(ok1, 0.0, NEG))
                    blk = s_raw[i * w:(i + 1) * w, jj * 2 * w:(jj + 1) * 2 * w]
                    blocks.append(blk + pt_ref[hh, t_idx] + mvec)
                row_blocks.append(jnp.concatenate(blocks, axis=1))
            s = jnp.concatenate(row_blocks, axis=0)
            sc = _dot_nt(qm, kc)
            m = jnp.maximum(jnp.max(s, axis=-1, keepdims=True), jnp.max(sc, axis=-1, keepdims=True))
            p = jnp.exp(s - m).astype(BF16)
            pc = jnp.exp(sc - m).astype(BF16)
            o = _dot(p, jnp.where(sel, vw, one)) + _dot(pc, jnp.where(sel, vc, one))
            den = pltpu.roll(o, NA_HEAD_DIM, 1)
            out = jnp.where(sel, o / den, out)
        o_ref[0, sub * nq:(sub + 1) * nq, :] = out.astype(o_ref.dtype)


def _na_call(q, k, v, pt, n_lat):
    b, s, _ = q.shape
    t = k.shape[1]
    rows = s // GRID_W
    nq = NA_STEP_BLOCKS * NA_QROWS * GRID_W
    hp = LANES // NA_HEAD_DIM
    return pl.pallas_call(
        functools.partial(_na_kernel, rows=rows, n_lat=n_lat),
        grid=(NA_WIDTH // LANES, b, s // nq),
        in_specs=[
            pl.BlockSpec((1, nq, LANES), lambda p, bb, rb: (bb, rb, p)),
            pl.BlockSpec((1, t, LANES), lambda p, bb, rb: (bb, 0, p)),
            pl.BlockSpec((1, t, LANES), lambda p, bb, rb: (bb, 0, p)),
            pl.BlockSpec((hp,) + pt.shape[1:], lambda p, bb, rb: (p, 0, 0, 0)),
        ],
        out_specs=pl.BlockSpec((1, nq, LANES), lambda p, bb, rb: (bb, rb, p)),
        out_shape=jax.ShapeDtypeStruct((b, s, NA_WIDTH), BF16),
        compiler_params=_cparams(3),
        name="nbr_attention",
    )(q, k, v, pt)


def _ssd_decays(dt_ref, prm_ref, row0, *, reverse):
    q = SSD_CHUNK
    dt = jax.nn.softplus(dt_ref[0, row0:row0 + q, :] + prm_ref[0:1, :])
    a = -dt * jnp.exp(prm_ref[1:2, :])
    ri = lax.broadcasted_iota(jnp.int32, (q, q), 0)
    ci = lax.broadcasted_iota(jnp.int32, (q, q), 1)
    tri = (ci >= ri) if reverse else (ci <= ri)
    ones = jnp.where(tri, 1.0, 0.0).astype(BF16)
    cs = jnp.zeros((q, LANES), F32)
    for part in _split3(a):
        cs = cs + _dot(ones, part)
    tot_row = cs[q - 1:q, :] if not reverse else cs[0:1, :]
    cs_t = cs.T
    dt_t = dt.T
    e_end_t = (jnp.exp(tot_row.T - cs_t) * dt_t).astype(BF16)
    seg_row = cs_t - jnp.log(dt_t)
    e_start = jnp.exp(cs)
    dec = jnp.exp(tot_row)
    return tri, cs, seg_row, e_end_t, e_start, dec


def _ssd_direction(xbc_ref, row0, decays, dsk_ref, state_ref, y_ref, *, reverse, emit_y):
    q = SSD_CHUNK
    tri, cs, seg_row, e_end_t, e_start, dec = decays
    lane_off = SSM_HEADS if reverse else 0
    lane = lax.broadcasted_iota(jnp.int32, (1, LANES), 1)
    lo = lane < SSM_HEAD_DIM
    rows = slice(row0, row0 + q)
    for g in range(SSM_GROUPS):
        bm = xbc_ref[0, rows, SSM_INNER + g * SSM_STATE:SSM_INNER + (g + 1) * SSM_STATE]
        cm = xbc_ref[0, rows, SSM_INNER + SSM_BC + g * SSM_STATE:SSM_INNER + SSM_BC + (g + 1) * SSM_STATE]
        bt = bm.astype(F32).T.astype(BF16)
        if emit_y:
            cb = _dot_nt(cm, bm).astype(BF16)
            cmf = cm.astype(F32)
        heads_per_group = SSM_HEADS // SSM_GROUPS
        for pp in range(heads_per_group // 2):
            pair = g * (heads_per_group // 2) + pp
            xs = xbc_ref[0, rows, pair * LANES:(pair + 1) * LANES]
            xs_lo = jnp.where(lo, xs, jnp.zeros_like(xs))
            xs_hi = jnp.where(lo, jnp.zeros_like(xs), xs)
            rhs_x = jnp.concatenate([xs_lo, xs_hi], axis=0)
            st = state_ref[pair]
            h0 = lane_off + 2 * pair
            btw = [bt * e_end_t[h0 + u:h0 + u + 1, :] for u in range(2)]
            upd = _dot(jnp.concatenate(btw, axis=1), rhs_x)
            if emit_y:
                stb = st.astype(BF16)
                st_lo = jnp.where(lo, stb, jnp.zeros_like(stb))
                st_hi = jnp.where(lo, jnp.zeros_like(stb), stb)
                lhs = []
                for u in range(2):
                    hcol = cs[:, h0 + u:h0 + u + 1]
                    seg = jnp.exp(jnp.where(tri, hcol - seg_row[h0 + u:h0 + u + 1, :], NEG))
                    lhs.append(cb * seg.astype(BF16))
                for u in range(2):
                    lhs.append((cmf * e_start[:, h0 + u:h0 + u + 1]).astype(BF16))
                y = _dot(jnp.concatenate(lhs, axis=1), jnp.concatenate([rhs_x, st_lo, st_hi], axis=0))
                y = y + dsk_ref[:, pair * LANES:(pair + 1) * LANES] * xs.astype(F32)
                y_ref[0, rows, pair * LANES:(pair + 1) * LANES] = y.astype(y_ref.dtype)
            dpair = jnp.where(lo, dec[:, h0:h0 + 1], dec[:, h0 + 1:h0 + 2])
            state_ref[pair] = st * dpair + upd


def _ssd_kernel(xf_ref, dtf_ref, xb_ref, dtb_ref, prm_ref, dsk_ref, yf_ref, yb_ref, sf_ref, sb_ref, *, n_ctx):
    s = pl.program_id(1)

    @pl.when(s == 0)
    def _():
        sf_ref[...] = jnp.zeros_like(sf_ref)
        sb_ref[...] = jnp.zeros_like(sb_ref)

    def run(emit_y):
        order_f = [c * SSD_CHUNK for c in range(SSD_STEP_CHUNKS)]
        order_b = order_f[::-1]
        dec_f = [_ssd_decays(dtf_ref, prm_ref, r, reverse=False) for r in order_f]
        dec_b = [_ssd_decays(dtb_ref, prm_ref, r, reverse=True) for r in order_b]
        for c in range(SSD_STEP_CHUNKS):
            _ssd_direction(xf_ref, order_f[c], dec_f[c], dsk_ref.at[0:1], sf_ref, yf_ref,
                           reverse=False, emit_y=emit_y)
            _ssd_direction(xb_ref, order_b[c], dec_b[c], dsk_ref.at[1:2], sb_ref, yb_ref,
                           reverse=True, emit_y=emit_y)

    @pl.when(s < n_ctx)
    def _():
        run(False)

    @pl.when(s >= n_ctx)
    def _():
        run(True)


def _ssd_call(xbc, dt, prm, dsk, n_lat):
    b, t, c = xbc.shape
    q = SSD_STEP_CHUNKS * SSD_CHUNK
    nc = t // q
    nl = n_lat // q
    n_ctx = nc - nl
    fwd = lambda bb, s: (bb, jnp.where(s < n_ctx, nl + s, s - n_ctx), 0)
    bwd = lambda bb, s: (bb, nc - 1 - s, 0)
    yf = lambda bb, s: (bb, jnp.maximum(s - n_ctx, 0), 0)
    yb = lambda bb, s: (bb, jnp.minimum(nc - 1 - s, nl - 1), 0)
    fixed = lambda bb, s: (0, 0)
    return pl.pallas_call(
        functools.partial(_ssd_kernel, n_ctx=n_ctx),
        grid=(b, nc),
        in_specs=[
            pl.BlockSpec((1, q, c), fwd),
            pl.BlockSpec((1, q, LANES), fwd),
            pl.BlockSpec((1, q, c), bwd),
            pl.BlockSpec((1, q, LANES), bwd),
            pl.BlockSpec((2, LANES), fixed),
            pl.BlockSpec((2, SSM_INNER), fixed),
        ],
        out_specs=[
            pl.BlockSpec((1, q, SSM_INNER), yf),
            pl.BlockSpec((1, q, SSM_INNER), yb),
        ],
        out_shape=[jax.ShapeDtypeStruct((b, n_lat, SSM_INNER), BF16)] * 2,
        scratch_shapes=[pltpu.VMEM((SSM_HEADS // 2, SSM_STATE, LANES), F32)] * 2,
        compiler_params=_cparams(2),
        name="ssd_scan",
    )(xbc, dt, xbc, dt, prm, dsk)


def _outproj_kernel(x_ref, att_ref, yf_ref, yb_ref, z_ref, gate_ref, ng_ref, gpost_ref, w_ref, o_ref):
    y = yf_ref[0].astype(F32) + yb_ref[0].astype(F32)
    yz = y * _silu(z_ref[0].astype(F32))
    gw = SSM_INNER // SSM_GROUPS
    parts = []
    for g in range(SSM_GROUPS):
        seg = yz[:, g * gw:(g + 1) * gw]
        parts.append(seg * _rms_scale(seg))
    ssm = (jnp.concatenate(parts, axis=-1) * ng_ref[...]).astype(BF16)
    o = _dot(att_ref[0], w_ref[0:NA_WIDTH, :]) + _dot(ssm, w_ref[NA_WIDTH:, :])
    o_ref[0] = _gated_norm_add(x_ref[0], o, gpost_ref[...], gate_ref[...])


def _outproj_call(x, att, yf, yb, z, mod4, norm4, ng, w):
    b, s, d = x.shape
    tm = ROW_TILE
    tile = lambda bb, i: (bb, i, 0)
    return pl.pallas_call(
        _outproj_kernel,
        grid=(b, s // tm),
        in_specs=[
            pl.BlockSpec((1, tm, d), tile),
            pl.BlockSpec((1, tm, NA_WIDTH), tile),
            pl.BlockSpec((1, tm, SSM_INNER), tile),
            pl.BlockSpec((1, tm, SSM_INNER), tile),
            pl.BlockSpec((1, tm, SSM_INNER), tile),
            _row_spec(d, lambda bb, i: (0, bb, 0, 2)),
            pl.BlockSpec((1, SSM_INNER), lambda bb, i: (0, 0)),
            _row_spec(d, lambda bb, i: (0, 1, 0, 0)),
            pl.BlockSpec((None,) + w.shape[1:], lambda bb, i: (0, 0, 0), pipeline_mode=pl.Buffered(1)),
        ],
        out_specs=pl.BlockSpec((1, tm, d), tile),
        out_shape=jax.ShapeDtypeStruct((b, s, d), F32),
        compiler_params=_cparams(2),
        name="outproj",
    )(x, att, yf, yb, z, mod4, ng, norm4, w)


def _ffn_kernel(x_ref, prev_ref, next_ref, sh_ref, sc_ref, gate_ref, gpre_ref, gpost_ref,
                wup_ref, cw_ref, cb_ref, wdn_ref, o_ref, h_scr, *, tn):
    i = pl.program_id(1)
    nt = pl.num_programs(1)
    tm = x_ref.shape[1]
    hidden = wdn_ref.shape[0]
    x = x_ref[0]

    def nm(xt):
        return _norm_mod(xt, gpre_ref[...], sc_ref[...], sh_ref[...]).astype(BF16)

    zero = jnp.zeros((), BF16)
    h_scr[0:HALO, :] = jnp.where(i > 0, nm(prev_ref[0]), zero)
    h_scr[HALO:HALO + tm, :] = nm(x)
    h_scr[HALO + tm:, :] = jnp.where(i < nt - 1, nm(next_ref[0]), zero)
    acc = jnp.zeros((tm, x.shape[1]), F32)
    for j in range(hidden // tn):
        u = _dot(h_scr[...], wup_ref[:, j * tn:(j + 1) * tn])
        v = _dot(h_scr[HALO:HALO + tm, :], wup_ref[:, hidden + j * tn:hidden + (j + 1) * tn])
        cv = _conv3_rows(u, cw_ref[:, j * tn:(j + 1) * tn], tm) + cb_ref[:, j * tn:(j + 1) * tn]
        gl = 0.5 * cv * (1.0 + lax.erf(cv * 0.7071067811865476)) * v
        acc = acc + _dot(gl.astype(BF16), wdn_ref[j * tn:(j + 1) * tn, :])
    o_ref[0] = _gated_norm_add(x, acc, gpost_ref[...], gate_ref[...])


def _ffn_call(x, layer, mod4, norm4, wup, cw, cb, wdn):
    b, s, d = x.shape
    hidden = wdn.shape[1]
    tm = FFN_ROW_TILE
    tile = lambda bb, i: (bb, i, 0)
    at_layer = lambda bb, i: (layer, 0, 0)
    return pl.pallas_call(
        functools.partial(_ffn_kernel, tn=FFN_COL_TILE),
        grid=(b, s // tm),
        in_specs=[pl.BlockSpec((1, tm, d), tile)] + _halo_specs(tm, d, s) + [
            _row_spec(d, lambda bb, i: (layer, bb, 0, 3)),
            _row_spec(d, lambda bb, i: (layer, bb, 0, 4)),
            _row_spec(d, lambda bb, i: (layer, bb, 0, 5)),
            _row_spec(d, lambda bb, i: (layer, 2, 0, 0)),
            _row_spec(d, lambda bb, i: (layer, 3, 0, 0)),
            pl.BlockSpec((None, d, 2 * hidden), at_layer, pipeline_mode=pl.Buffered(1)),
            pl.BlockSpec((None,) + cw.shape[1:], at_layer),
            pl.BlockSpec((None, 1, hidden), at_layer),
            pl.BlockSpec((None, hidden, d), at_layer, pipeline_mode=pl.Buffered(1)),
        ],
        out_specs=pl.BlockSpec((1, tm, d), tile),
        out_shape=jax.ShapeDtypeStruct((b, s, d), F32),
        scratch_shapes=[pltpu.VMEM((tm + 2 * HALO, d), BF16)],
        compiler_params=_cparams(2),
        name="conv_ffn",
    )(x, x, x, mod4, mod4, mod4, norm4, norm4, wup, cw, cb, wdn)


def _window_sum(h_ext, w, rows):
    n = h_ext.shape[0]
    acc = h_ext
    span = 1
    while span < w:
        acc = acc + pltpu.roll(acc, span, 0)
        span *= 2
    shift = w // 2 - 1
    if shift:
        acc = pltpu.roll(acc, n - shift, 0)
    return acc[HALO:HALO + rows]


def _pool_kernel(x_ref, prev_ref, next_ref, sh_ref, sc_ref, gate_ref, gpre_ref, gpost_ref,
                 pw_ref, pb_ref, ps_ref, o_ref, h_scr, *, seq):
    i = pl.program_id(1)
    nt = pl.num_programs(1)
    ts = x_ref.shape[1]
    x = x_ref[0]
    gwidth = pw_ref.shape[1]

    def nm(xt):
        return _norm_mod(xt, gpre_ref[...], sc_ref[...], sh_ref[...])

    h_scr[0:HALO, :] = jnp.where(i > 0, nm(prev_ref[0]), 0.0)
    h_scr[HALO:HALO + ts, :] = nm(x)
    h_scr[HALO + ts:, :] = jnp.where(i < nt - 1, nm(next_ref[0]), 0.0)
    t = i * ts + lax.broadcasted_iota(jnp.int32, (ts, 1), 0)
    ys = []
    for gi, w in enumerate(POOL_WINDOWS):
        cols = slice(gi * gwidth, (gi + 1) * gwidth)
        h_ext = h_scr[:, cols]
        cnt = (jnp.minimum(t + w // 2, seq) - jnp.maximum(t - w // 2, 0)).astype(F32)
        pooled = _window_sum(h_ext, w, ts) / cnt - h_ext[HALO:HALO + ts]
        ys.append(_dot(pooled.astype(BF16), pw_ref[gi]) + pb_ref[:, cols])
    y = jnp.concatenate(ys, axis=-1) * ps_ref[...]
    o_ref[0] = _gated_norm_add(x, y, gpost_ref[...], gate_ref[...])


def _pool_call(x, layer, mod4, norm4, pw, pb, ps):
    b, s, d = x.shape
    ts = ROW_TILE
    tile = lambda bb, i: (bb, i, 0)
    fixed = lambda bb, i: (0, 0)
    return pl.pallas_call(
        functools.partial(_pool_kernel, seq=s),
        grid=(b, s // ts),
        in_specs=[pl.BlockSpec((1, ts, d), tile)] + _halo_specs(ts, d, s) + [
            _row_spec(d, lambda bb, i: (layer, bb, 0, 0)),
            _row_spec(d, lambda bb, i: (layer, bb, 0, 1)),
            _row_spec(d, lambda bb, i: (layer, bb, 0, 2)),
            _row_spec(d, lambda bb, i: (layer, 0, 0, 0)),
            _row_spec(d, lambda bb, i: (layer, 1, 0, 0)),
            pl.BlockSpec(pw.shape, lambda bb, i: (0, 0, 0)),
            pl.BlockSpec((1, d), fixed),
            pl.BlockSpec((1, d), fixed),
        ],
        out_specs=pl.BlockSpec((1, ts, d), tile),
        out_shape=jax.ShapeDtypeStruct((b, s, d), F32),
        scratch_shapes=[pltpu.VMEM((ts + 2 * HALO, d), F32)],
        compiler_params=_cparams(2),
        name="pool_mixer",
    )(x, x, x, mod4, mod4, mod4, norm4, norm4, pw, pb, ps)


def kernel(x, c, ctx, c_ctx, ada_w, ada_b, norm_g, w_in, w_out, na_rpb, ssm_conv_w, ssm_conv_b, ssm_a_log, ssm_dt_bias, ssm_d, ssm_norm_g, pool_w, pool_b, pool_scale, ffn_w_up, ffn_conv_w, ffn_conv_b, ffn_w_down):
    b, s, d = x.shape
    l = ctx.shape[1]
    depth = ada_w.shape[0]
    hidden = ffn_w_down.shape[1]
    rows = s // GRID_W
    assert depth == 2 and l % (SSD_STEP_CHUNKS * SSD_CHUNK) == 0 and s % l == 0 and s % (NA_STEP_BLOCKS * NA_QROWS * GRID_W) == 0
    assert s % FFN_ROW_TILE == 0 and s % ROW_TILE == 0 and hidden % FFN_COL_TILE == 0
    assert rows >= NA_KROWS and 2 * SSM_HEADS <= LANES
    assert NA_DOFF + 2 * NA_WIN_ROWS - 1 <= NA_NTILES + 1 and NA_KROWS + NA_QROWS + NA_DOFF <= NA_NTILES + 1

    rows_c = -(-(b + 1) // 8) * 8
    c_ext = jnp.zeros((rows_c, d), F32).at[:b].set(c).at[b].set(c_ctx)
    mod4 = _ada_call(c_ext, ada_w, ada_b).reshape(depth, rows_c, 1, 6 * d)
    norm4 = norm_g.reshape(depth, 4, 1, d)
    wup = ffn_w_up.astype(BF16)
    wdn = ffn_w_down.astype(BF16)
    ffn_cb = ffn_conv_b.reshape(depth, 1, hidden)

    w_cat = jnp.pad(w_in, ((0, 0), (0, 0), (0, LANES - 2 * SSM_HEADS))).astype(BF16)
    q, z, k, v, xbc, dt_raw = _inproj_call(x, ctx, mod4, norm4, w_cat, ssm_conv_w,
                                           ssm_conv_b.reshape(1, 1, SSM_CONV_DIM))
    att = _na_call(q, k, v, _na_bias_tiles(na_rpb[0]), s)
    pad = LANES - 2 * SSM_HEADS
    prm = jnp.stack([jnp.pad(ssm_dt_bias[0].reshape(-1), (0, pad)), jnp.pad(ssm_a_log[0].reshape(-1), (0, pad))])
    dsk = jnp.repeat(ssm_d[0], SSM_HEAD_DIM, axis=1)
    y_f, y_b = _ssd_call(xbc, dt_raw, prm, dsk, s)
    x = _outproj_call(x, att, y_f, y_b, z, mod4, norm4, ssm_norm_g.reshape(1, SSM_INNER), w_out.astype(BF16))
    x = _ffn_call(x, 0, mod4, norm4, wup, ffn_conv_w, ffn_cb, wdn)

    x = _pool_call(x, 1, mod4, norm4, pool_w[0].astype(BF16), pool_b.reshape(1, d), pool_scale.reshape(1, d))
    x = _ffn_call(x, 1, mod4, norm4, wup, ffn_conv_w, ffn_cb, wdn)
    return x
```

```python
import functools

import numpy as np
import jax
import jax.numpy as jnp
from jax import lax
from jax.experimental import pallas as pl
from jax.experimental.pallas import tpu as pltpu

F32 = jnp.float32
BF16 = jnp.bfloat16

GRID_W = 64
NA_HEADS = 8
NA_HEAD_DIM = 64
NA_WIDTH = NA_HEADS * NA_HEAD_DIM
NA_WIN_ROWS = 8
NA_WIN_COLS = 16
SSM_HEADS = 16
SSM_HEAD_DIM = 64
SSM_INNER = SSM_HEADS * SSM_HEAD_DIM
SSM_GROUPS = 4
SSM_STATE = 128
SSD_CHUNK = 128
SSD_STEP_CHUNKS = 2
SSM_BC = SSM_GROUPS * SSM_STATE
SSM_CONV_DIM = SSM_INNER + 2 * SSM_BC
POOL_WINDOWS = (2, 4, 8, 16)
RMS_EPS = 1e-6

LANES = 128
HALO = 16
NA_QROWS = 8
NA_KROWS = 16
NA_STEP_BLOCKS = 2
NA_DOFF = NA_QROWS
NA_NTILES = 32
NEG = -1e30
VMEM_LIMIT = 56 * 1024 * 1024
ROW_TILE = 512
FFN_ROW_TILE = 1024
FFN_COL_TILE = 256


def _cparams(n_axes):
    return pltpu.CompilerParams(dimension_semantics=("arbitrary",) * n_axes, vmem_limit_bytes=VMEM_LIMIT)


def _rms_scale(x):
    return lax.rsqrt(jnp.mean(x * x, axis=-1, keepdims=True) + RMS_EPS)


def _norm_mod(x, g, sc, sh):
    return x * _rms_scale(x) * (g * (1.0 + sc)) + sh


def _gated_norm_add(x, y, g, gate):
    return x + y * _rms_scale(y) * (g * gate)


def _silu(x):
    return x * jax.nn.sigmoid(x)


def _dot(a, b):
    return jnp.dot(a, b, preferred_element_type=F32)


def _dot_nt(a, b):
    return lax.dot_general(a, b, (((1,), (1,)), ((), ())), preferred_element_type=F32)


def _split3(a):
    hi = a.astype(BF16)
    r = a - hi.astype(F32)
    mid = r.astype(BF16)
    lo = (r - mid.astype(F32)).astype(BF16)
    return hi, mid, lo


def _conv3_rows(u, w_rows, rows):
    n = u.shape[0]
    u_prev = pltpu.roll(u, 1, 0)[HALO:HALO + rows]
    u_next = pltpu.roll(u, n - 1, 0)[HALO:HALO + rows]
    return u_prev * w_rows[0:1] + u[HALO:HALO + rows] * w_rows[1:2] + u_next * w_rows[2:3]


def _row_spec(d, index):
    return pl.BlockSpec((None, None, 1, d), index)


def _halo_specs(tm, d, s):
    rh = tm // HALO
    return [
        pl.BlockSpec((1, HALO, d), lambda bb, i: (bb, jnp.maximum(i * rh - 1, 0), 0)),
        pl.BlockSpec((1, HALO, d), lambda bb, i: (bb, jnp.minimum((i + 1) * rh, s // HALO - 1), 0)),
    ]


def _ada_kernel(c_ref, w_ref, b_ref, o_ref):
    s = _silu(c_ref[...])
    w = w_ref[0]
    acc = jnp.zeros(o_ref.shape[1:], F32)
    s_parts = _split3(s)
    w_parts = _split3(w)
    for si, sp in enumerate(s_parts):
        for wi, wp in enumerate(w_parts):
            if si + wi <= 2:
                acc = acc + _dot(sp, wp)
    o_ref[0] = acc + b_ref[0]


def _ada_call(c_ext, ada_w, ada_b):
    depth, d, n = ada_w.shape
    r = c_ext.shape[0]
    tn = 768
    return pl.pallas_call(
        _ada_kernel,
        grid=(depth, n // tn),
        in_specs=[
            pl.BlockSpec((r, d), lambda l, j: (0, 0)),
            pl.BlockSpec((1, d, tn), lambda l, j: (l, 0, j)),
            pl.BlockSpec((1, 1, tn), lambda l, j: (l, 0, j)),
        ],
        out_specs=pl.BlockSpec((1, r, tn), lambda l, j: (l, 0, j)),
        out_shape=jax.ShapeDtypeStruct((depth, r, n), F32),
        compiler_params=_cparams(2),
        name="ada_mod",
    )(c_ext, ada_w, ada_b.reshape(depth, 1, n))


def _inproj_body(h_scr, w_ref, cw_ref, cb_ref, qz_refs, k_ref, v_ref, xbc_ref, dt_ref, tm):
    hb = h_scr[HALO:HALO + tm, :]
    if qz_refs is not None:
        q_ref, z_ref = qz_refs
        q_ref[0] = _dot(hb, w_ref[:, 0:NA_WIDTH]).astype(q_ref.dtype)
        z_ref[0] = _dot(hb, w_ref[:, NA_WIDTH:NA_WIDTH + SSM_INNER]).astype(z_ref.dtype)
    o = NA_WIDTH + SSM_INNER
    k_ref[0] = _dot(hb, w_ref[:, o:o + NA_WIDTH]).astype(k_ref.dtype)
    o += NA_WIDTH
    v_ref[0] = _dot(hb, w_ref[:, o:o + NA_WIDTH]).astype(v_ref.dtype)
    o += NA_WIDTH
    cblk = 512
    for c0 in range(0, SSM_CONV_DIM, cblk):
        u = _dot(h_scr[...], w_ref[:, o + c0:o + c0 + cblk])
        y = _conv3_rows(u, cw_ref[:, c0:c0 + cblk], tm) + cb_ref[:, c0:c0 + cblk]
        xbc_ref[0, :, c0:c0 + cblk] = _silu(y).astype(xbc_ref.dtype)
    o += SSM_CONV_DIM
    dt_ref[0] = _dot(hb, w_ref[:, o:o + LANES]).astype(dt_ref.dtype)


def _inproj_lat_kernel(x_ref, prev_ref, next_ref, sh_ref, sc_ref, g_ref, w_ref, cw_ref, cb_ref,
                       q_ref, z_ref, k_ref, v_ref, xbc_ref, dt_ref, h_scr):
    i = pl.program_id(1)
    nt = pl.num_programs(1)
    tm = x_ref.shape[1]

    def nm(xt):
        return _norm_mod(xt, g_ref[...], sc_ref[...], sh_ref[...]).astype(BF16)

    zero = jnp.zeros((), BF16)
    h_scr[0:HALO, :] = jnp.where(i > 0, nm(prev_ref[0]), zero)
    h_scr[HALO:HALO + tm, :] = nm(x_ref[0])
    h_scr[HALO + tm:, :] = jnp.where(i < nt - 1, nm(next_ref[0]), zero)
    _inproj_body(h_scr, w_ref, cw_ref, cb_ref, (q_ref, z_ref), k_ref, v_ref, xbc_ref, dt_ref, tm)


def _inproj_ctx_kernel(x_ref, sh_ref, sc_ref, g_ref, w_ref, cw_ref, cb_ref, k_ref, v_ref, xbc_ref, dt_ref, h_scr):
    tm = x_ref.shape[1]
    h_scr[0:HALO, :] = jnp.zeros((HALO, h_scr.shape[1]), BF16)
    h_scr[HALO:HALO + tm, :] = _norm_mod(x_ref[0], g_ref[...], sc_ref[...], sh_ref[...]).astype(BF16)
    h_scr[HALO + tm:, :] = jnp.zeros((HALO, h_scr.shape[1]), BF16)
    _inproj_body(h_scr, w_ref, cw_ref, cb_ref, None, k_ref, v_ref, xbc_ref, dt_ref, tm)


def _inproj_call(x, ctx, mod4, norm4, w, cw, cb):
    b, s, d = x.shape
    l = ctx.shape[1]
    tm = ROW_TILE
    n = w.shape[-1]
    tile = lambda bb, i: (bb, i, 0)
    fixed = lambda bb, i: (0, 0, 0)
    weights = [
        pl.BlockSpec((None, d, n), fixed, pipeline_mode=pl.Buffered(1)),
        pl.BlockSpec((None,) + cw.shape[1:], fixed),
        pl.BlockSpec((None,) + cb.shape[1:], fixed),
    ]
    widths = (NA_WIDTH, NA_WIDTH, SSM_CONV_DIM, LANES)
    dtypes = (BF16, BF16, BF16, F32)
    q, z, k, v, xbc, dt = pl.pallas_call(
        _inproj_lat_kernel,
        grid=(b, s // tm),
        in_specs=[pl.BlockSpec((1, tm, d), tile)] + _halo_specs(tm, d, s) + [
            _row_spec(d, lambda bb, i: (0, bb, 0, 0)),
            _row_spec(d, lambda bb, i: (0, bb, 0, 1)),
            _row_spec(d, lambda bb, i: (0, 0, 0, 0)),
        ] + weights,
        out_specs=[pl.BlockSpec((1, tm, NA_WIDTH), tile), pl.BlockSpec((1, tm, SSM_INNER), tile)]
        + [pl.BlockSpec((1, tm, wd), tile) for wd in widths],
        out_shape=[jax.ShapeDtypeStruct((b, s, NA_WIDTH), BF16), jax.ShapeDtypeStruct((b, s, SSM_INNER), BF16)]
        + [jax.ShapeDtypeStruct((b, s, wd), dt_) for wd, dt_ in zip(widths, dtypes)],
        scratch_shapes=[pltpu.VMEM((tm + 2 * HALO, d), BF16)],
        compiler_params=_cparams(2),
        name="inproj",
    )(x, x, x, mod4, mod4, norm4, w, cw, cb)

    kc, vc, xbc_c, dt_c = pl.pallas_call(
        _inproj_ctx_kernel,
        grid=(b, 1),
        in_specs=[
            pl.BlockSpec((1, l, d), tile),
            _row_spec(d, lambda bb, i: (0, b, 0, 0)),
            _row_spec(d, lambda bb, i: (0, b, 0, 1)),
            _row_spec(d, lambda bb, i: (0, 0, 0, 0)),
        ] + weights,
        out_specs=[pl.BlockSpec((1, l, wd), tile) for wd in widths],
        out_shape=[jax.ShapeDtypeStruct((b, l, wd), dt_) for wd, dt_ in zip(widths, dtypes)],
        scratch_shapes=[pltpu.VMEM((l + 2 * HALO, d), BF16)],
        compiler_params=_cparams(2),
        name="inproj_ctx",
    )(ctx, mod4, mod4, norm4, w, cw, cb)
    return (q, z, k, v, xbc, dt), (kc, vc, xbc_c, dt_c)


def _na_bias_tiles(rpb):
    h, n_rel_r, n_rel_c = rpb.shape
    w = GRID_W
    c = np.arange(w)[:, None]
    kc = np.arange(w)[None, :]
    cs = np.clip(c - NA_WIN_COLS // 2, 0, w - NA_WIN_COLS)
    col_ok = (kc >= cs) & (kc < cs + NA_WIN_COLS)
    pick = ((kc - c + NA_WIN_COLS - 1)[None] == np.arange(n_rel_c)[:, None, None]) & col_ok[None]
    t = jnp.einsum("hrd,dck->hrck", rpb, jnp.asarray(pick, F32), precision=lax.Precision.HIGHEST)
    t = jnp.where(jnp.asarray(col_ok)[None, None], t, NEG)
    n_after = NA_NTILES + 1 - NA_DOFF - n_rel_r
    tz = jnp.concatenate(
        [jnp.full((h, NA_DOFF, w, w), NEG, F32), t, jnp.full((h, n_after, w, w), NEG, F32)], axis=1)
    return jnp.concatenate([tz[:, :-1], tz[:, 1:]], axis=-1)


def _na_window_rows(kind, rows):
    r0 = {"first": 0, "interior": NA_QROWS, "last": rows - NA_QROWS}[kind]
    k0 = min(max(r0 - NA_WIN_ROWS // 2, 0), rows - NA_KROWS)
    los = [min(max(r0 + i - NA_WIN_ROWS // 2, 0), rows - NA_WIN_ROWS) - k0 for i in range(NA_QROWS)]
    return k0 - r0, los


def _na_block(q_ref, k_ref, v_ref, kc_ref, vc_ref, pt_ref, o_ref, sub, kind, rows):
    nk = NA_KROWS * GRID_W
    nq = NA_QROWS * GRID_W
    w = GRID_W
    delta, los = _na_window_rows(kind, rows)
    r0 = (pl.program_id(2) * NA_STEP_BLOCKS + sub) * NA_QROWS
    kstart = pl.multiple_of((r0 + delta) * GRID_W, GRID_W)
    q = q_ref[0, sub * nq:(sub + 1) * nq, :]
    kw = k_ref[0, pl.ds(kstart, nk), :]
    vw = v_ref[0, pl.ds(kstart, nk), :]
    kc = kc_ref[0]
    vc = vc_ref[0]
    lane = lax.broadcasted_iota(jnp.int32, (1, LANES), 1)
    one = jnp.ones((), v_ref.dtype)
    scale = jnp.asarray(NA_HEAD_DIM ** -0.5, q_ref.dtype)
    npairs = NA_KROWS // 2
    out = jnp.zeros((nq, LANES), F32)
    for hh in range(LANES // NA_HEAD_DIM):
        sel = jnp.logical_and(lane >= hh * NA_HEAD_DIM, lane < (hh + 1) * NA_HEAD_DIM)
        qm = jnp.where(sel, q, jnp.zeros_like(q)) * scale
        s_raw = _dot_nt(qm, kw)
        sc = _dot_nt(qm, kc)
        p_rows, pc_rows = [], []
        for i in range(NA_QROWS):
            lo = los[i]
            jj_lo, jj_hi = lo // 2, (lo + NA_WIN_ROWS - 1) // 2
            blocks = []
            for jj in range(jj_lo, jj_hi + 1):
                t_idx = delta + 2 * jj - i + (NA_WIN_ROWS - 1) + NA_DOFF
                blk = s_raw[i * w:(i + 1) * w, jj * 2 * w:(jj + 1) * 2 * w] + pt_ref[hh, t_idx]
                left_out, right_out = 2 * jj < lo, 2 * jj + 1 >= lo + NA_WIN_ROWS
                if left_out or right_out:
                    blk = blk + jnp.where(lane < w, NEG if left_out else 0.0, NEG if right_out else 0.0)
                blocks.append(blk)
            s_i = jnp.concatenate(blocks, axis=1)
            sc_i = sc[i * w:(i + 1) * w, :]
            m = jnp.maximum(jnp.max(s_i, axis=-1, keepdims=True), jnp.max(sc_i, axis=-1, keepdims=True))
            pieces = [jnp.exp(s_i - m).astype(BF16)]
            if jj_lo:
                pieces.insert(0, jnp.zeros((w, jj_lo * 2 * w), BF16))
            if jj_hi + 1 < npairs:
                pieces.append(jnp.zeros((w, (npairs - jj_hi - 1) * 2 * w), BF16))
            p_rows.append(jnp.concatenate(pieces, axis=1) if len(pieces) > 1 else pieces[0])
            pc_rows.append(jnp.exp(sc_i - m).astype(BF16))
        p = jnp.concatenate(p_rows, axis=0)
        pc = jnp.concatenate(pc_rows, axis=0)
        o = _dot(p, jnp.where(sel, vw, one)) + _dot(pc, jnp.where(sel, vc, one))
        den = pltpu.roll(o, NA_HEAD_DIM, 1)
        out = jnp.where(sel, o / den, out)
    o_ref[0, sub * nq:(sub + 1) * nq, :] = out.astype(o_ref.dtype)


def _na_kernel(q_ref, k_ref, v_ref, kc_ref, vc_ref, pt_ref, o_ref, *, rows):
    step = pl.program_id(2)
    nsteps = rows // (NA_QROWS * NA_STEP_BLOCKS)

    def run(kinds):
        for sub, kind in enumerate(kinds):
            _na_block(q_ref, k_ref, v_ref, kc_ref, vc_ref, pt_ref, o_ref, sub, kind, rows)

    inner = ("interior",) * (NA_STEP_BLOCKS - 1)
    if nsteps == 1:
        run(("first",) + ("interior",) * (NA_STEP_BLOCKS - 2) + ("last",))
    else:
        pl.when(step == 0)(lambda: run(("first",) + inner))
        pl.when(jnp.logical_and(step > 0, step < nsteps - 1))(lambda: run(inner + ("interior",)))
        pl.when(step == nsteps - 1)(lambda: run(inner + ("last",)))


def _na_call(q, k, v, kc, vc, pt):
    b, s, _ = q.shape
    l = kc.shape[1]
    rows = s // GRID_W
    nq = NA_STEP_BLOCKS * NA_QROWS * GRID_W
    hp = LANES // NA_HEAD_DIM
    per_batch = lambda p, bb, rb: (bb, 0, p)
    return pl.pallas_call(
        functools.partial(_na_kernel, rows=rows),
        grid=(NA_WIDTH // LANES, b, s // nq),
        in_specs=[
            pl.BlockSpec((1, nq, LANES), lambda p, bb, rb: (bb, rb, p)),
            pl.BlockSpec((1, s, LANES), per_batch),
            pl.BlockSpec((1, s, LANES), per_batch),
            pl.BlockSpec((1, l, LANES), per_batch),
            pl.BlockSpec((1, l, LANES), per_batch),
            pl.BlockSpec((hp,) + pt.shape[1:], lambda p, bb, rb: (p, 0, 0, 0)),
        ],
        out_specs=pl.BlockSpec((1, nq, LANES), lambda p, bb, rb: (bb, rb, p)),
        out_shape=jax.ShapeDtypeStruct((b, s, NA_WIDTH), BF16),
        compiler_params=_cparams(3),
        name="nbr_attention",
    )(q, k, v, kc, vc, pt)


def _ssd_decays(dt_ref, prm_ref, row0, *, reverse):
    q = SSD_CHUNK
    dt = jax.nn.softplus(dt_ref[0, row0:row0 + q, :] + prm_ref[0:1, :])
    a = -dt * jnp.exp(prm_ref[1:2, :])
    ri = lax.broadcasted_iota(jnp.int32, (q, q), 0)
    ci = lax.broadcasted_iota(jnp.int32, (q, q), 1)
    tri = (ci >= ri) if reverse else (ci <= ri)
    ones = jnp.where(tri, 1.0, 0.0).astype(BF16)
    cs = jnp.zeros((q, LANES), F32)
    for part in _split3(a):
        cs = cs + _dot(ones, part)
    tot_row = cs[q - 1:q, :] if not reverse else cs[0:1, :]
    cs_t = cs.T
    dt_t = dt.T
    e_end_t = (jnp.exp(tot_row.T - cs_t) * dt_t).astype(BF16)
    seg_row = cs_t - jnp.log(dt_t)
    e_start = jnp.exp(cs)
    dec = jnp.exp(tot_row)
    return tri, cs, seg_row, e_end_t, e_start, dec


def _ssd_direction(xbc_ref, row0, decays, dsk_ref, state_ref, y_ref, *, reverse, emit_y):
    q = SSD_CHUNK
    tri, cs, seg_row, e_end_t, e_start, dec = decays
    lane_off = SSM_HEADS if reverse else 0
    lane = lax.broadcasted_iota(jnp.int32, (1, LANES), 1)
    lo = lane < SSM_HEAD_DIM
    rows = slice(row0, row0 + q)
    for g in range(SSM_GROUPS):
        bm = xbc_ref[0, rows, SSM_INNER + g * SSM_STATE:SSM_INNER + (g + 1) * SSM_STATE]
        cm = xbc_ref[0, rows, SSM_INNER + SSM_BC + g * SSM_STATE:SSM_INNER + SSM_BC + (g + 1) * SSM_STATE]
        bt = bm.astype(F32).T.astype(BF16)
        if emit_y:
            cb = _dot_nt(cm, bm).astype(BF16)
            cmf = cm.astype(F32)
        heads_per_group = SSM_HEADS // SSM_GROUPS
        for pp in range(heads_per_group // 2):
            pair = g * (heads_per_group // 2) + pp
            xs = xbc_ref[0, rows, pair * LANES:(pair + 1) * LANES]
            xs_lo = jnp.where(lo, xs, jnp.zeros_like(xs))
            xs_hi = jnp.where(lo, jnp.zeros_like(xs), xs)
            rhs_x = jnp.concatenate([xs_lo, xs_hi], axis=0)
            st = state_ref[pair]
            h0 = lane_off + 2 * pair
            btw = [bt * e_end_t[h0 + u:h0 + u + 1, :] for u in range(2)]
            upd = _dot(jnp.concatenate(btw, axis=1), rhs_x)
            if emit_y:
                stb = st.astype(BF16)
                st_lo = jnp.where(lo, stb, jnp.zeros_like(stb))
                st_hi = jnp.where(lo, jnp.zeros_like(stb), stb)
                lhs = []
                for u in range(2):
                    hcol = cs[:, h0 + u:h0 + u + 1]
                    seg = jnp.exp(jnp.where(tri, hcol - seg_row[h0 + u:h0 + u + 1, :], NEG))
                    lhs.append(cb * seg.astype(BF16))
                for u in range(2):
                    lhs.append((cmf * e_start[:, h0 + u:h0 + u + 1]).astype(BF16))
                y = _dot(jnp.concatenate(lhs, axis=1), jnp.concatenate([rhs_x, st_lo, st_hi], axis=0))
                y = y + dsk_ref[:, pair * LANES:(pair + 1) * LANES] * xs.astype(F32)
                y_ref[0, rows, pair * LANES:(pair + 1) * LANES] = y.astype(y_ref.dtype)
            dpair = jnp.where(lo, dec[:, h0:h0 + 1], dec[:, h0 + 1:h0 + 2])
            state_ref[pair] = st * dpair + upd


def _ssd_kernel(xf_ref, dtf_ref, xb_ref, dtb_ref, xc_ref, dtc_ref, prm_ref, dsk_ref, yf_ref, yb_ref,
                sf_ref, sb_ref, *, n_ctx):
    s = pl.program_id(1)
    step = SSD_STEP_CHUNKS * SSD_CHUNK

    @pl.when(s == 0)
    def _():
        sf_ref[...] = jnp.zeros_like(sf_ref)
        sb_ref[...] = jnp.zeros_like(sb_ref)

    def run(x_f, dt_f, x_b, dt_b, base_f, base_b, emit_y):
        order_f = [c * SSD_CHUNK for c in range(SSD_STEP_CHUNKS)]
        order_b = order_f[::-1]
        dec_f = [_ssd_decays(dt_f, prm_ref, base_f + r, reverse=False) for r in order_f]
        dec_b = [_ssd_decays(dt_b, prm_ref, base_b + r, reverse=True) for r in order_b]
        for c in range(SSD_STEP_CHUNKS):
            _ssd_direction(x_f, base_f + order_f[c], dec_f[c], dsk_ref.at[0:1], sf_ref, yf_ref,
                           reverse=False, emit_y=emit_y)
            _ssd_direction(x_b, base_b + order_b[c], dec_b[c], dsk_ref.at[1:2], sb_ref, yb_ref,
                           reverse=True, emit_y=emit_y)

    for j in range(n_ctx):
        pl.when(s == j)(functools.partial(run, xc_ref, dtc_ref, xc_ref, dtc_ref,
                                          j * step, (n_ctx - 1 - j) * step, False))

    @pl.when(s >= n_ctx)
    def _():
        run(xf_ref, dtf_ref, xb_ref, dtb_ref, 0, 0, True)


def _ssd_call(xbc, dt, xbc_c, dt_c, prm, dsk):
    b, n_lat, c = xbc.shape
    l = xbc_c.shape[1]
    q = SSD_STEP_CHUNKS * SSD_CHUNK
    nl = n_lat // q
    n_ctx = l // q
    fwd = lambda bb, s: (bb, jnp.maximum(s - n_ctx, 0), 0)
    bwd = lambda bb, s: (bb, jnp.minimum(nl - 1 - (s - n_ctx), nl - 1), 0)
    whole = lambda bb, s: (bb, 0, 0)
    fixed = lambda bb, s: (0, 0)
    return pl.pallas_call(
        functools.partial(_ssd_kernel, n_ctx=n_ctx),
        grid=(b, n_ctx + nl),
        in_specs=[
            pl.BlockSpec((1, q, c), fwd),
            pl.BlockSpec((1, q, LANES), fwd),
            pl.BlockSpec((1, q, c), bwd),
            pl.BlockSpec((1, q, LANES), bwd),
            pl.BlockSpec((1, l, c), whole),
            pl.BlockSpec((1, l, LANES), whole),
            pl.BlockSpec((2, LANES), fixed),
            pl.BlockSpec((2, SSM_INNER), fixed),
        ],
        out_specs=[
            pl.BlockSpec((1, q, SSM_INNER), fwd),
            pl.BlockSpec((1, q, SSM_INNER), bwd),
        ],
        out_shape=[jax.ShapeDtypeStruct((b, n_lat, SSM_INNER), BF16)] * 2,
        scratch_shapes=[pltpu.VMEM((SSM_HEADS // 2, SSM_STATE, LANES), F32)] * 2,
        compiler_params=_cparams(2),
        name="ssd_scan",
    )(xbc, dt, xbc, dt, xbc_c, dt_c, prm, dsk)


def _outproj_kernel(x_ref, att_ref, yf_ref, yb_ref, z_ref, gate_ref, ng_ref, gpost_ref, w_ref, o_ref):
    y = yf_ref[0].astype(F32) + yb_ref[0].astype(F32)
    yz = y * _silu(z_ref[0].astype(F32))
    gw = SSM_INNER // SSM_GROUPS
    parts = []
    for g in range(SSM_GROUPS):
        seg = yz[:, g * gw:(g + 1) * gw]
        parts.append(seg * _rms_scale(seg))
    ssm = (jnp.concatenate(parts, axis=-1) * ng_ref[...]).astype(BF16)
    o = _dot(att_ref[0], w_ref[0:NA_WIDTH, :]) + _dot(ssm, w_ref[NA_WIDTH:, :])
    o_ref[0] = _gated_norm_add(x_ref[0], o, gpost_ref[...], gate_ref[...])


def _outproj_call(x, att, yf, yb, z, mod4, norm4, ng, w):
    b, s, d = x.shape
    tm = ROW_TILE
    tile = lambda bb, i: (bb, i, 0)
    return pl.pallas_call(
        _outproj_kernel,
        grid=(b, s // tm),
        in_specs=[
            pl.BlockSpec((1, tm, d), tile),
            pl.BlockSpec((1, tm, NA_WIDTH), tile),
            pl.BlockSpec((1, tm, SSM_INNER), tile),
            pl.BlockSpec((1, tm, SSM_INNER), tile),
            pl.BlockSpec((1, tm, SSM_INNER), tile),
            _row_spec(d, lambda bb, i: (0, bb, 0, 2)),
            pl.BlockSpec((1, SSM_INNER), lambda bb, i: (0, 0)),
            _row_spec(d, lambda bb, i: (0, 1, 0, 0)),
            pl.BlockSpec((None,) + w.shape[1:], lambda bb, i: (0, 0, 0), pipeline_mode=pl.Buffered(1)),
        ],
        out_specs=pl.BlockSpec((1, tm, d), tile),
        out_shape=jax.ShapeDtypeStruct((b, s, d), F32),
        compiler_params=_cparams(2),
        name="outproj",
    )(x, att, yf, yb, z, mod4, ng, norm4, w)


def _ffn_kernel(x_ref, prev_ref, next_ref, sh_ref, sc_ref, gate_ref, gpre_ref, gpost_ref,
                wup_ref, cw_ref, cb_ref, wdn_ref, o_ref, h_scr, *, tn):
    i = pl.program_id(1)
    nt = pl.num_programs(1)
    tm = x_ref.shape[1]
    hidden = wdn_ref.shape[0]
    x = x_ref[0]

    def nm(xt):
        return _norm_mod(xt, gpre_ref[...], sc_ref[...], sh_ref[...]).astype(BF16)

    zero = jnp.zeros((), BF16)
    h_scr[0:HALO, :] = jnp.where(i > 0, nm(prev_ref[0]), zero)
    h_scr[HALO:HALO + tm, :] = nm(x)
    h_scr[HALO + tm:, :] = jnp.where(i < nt - 1, nm(next_ref[0]), zero)
    acc = jnp.zeros((tm, x.shape[1]), F32)
    for j in range(hidden // tn):
        u = _dot(h_scr[...], wup_ref[:, j * tn:(j + 1) * tn])
        v = _dot(h_scr[HALO:HALO + tm, :], wup_ref[:, hidden + j * tn:hidden + (j + 1) * tn])
        cv = _conv3_rows(u, cw_ref[:, j * tn:(j + 1) * tn], tm) + cb_ref[:, j * tn:(j + 1) * tn]
        gl = 0.5 * cv * (1.0 + lax.erf(cv * 0.7071067811865476)) * v
        acc = acc + _dot(gl.astype(BF16), wdn_ref[j * tn:(j + 1) * tn, :])
    o_ref[0] = _gated_norm_add(x, acc, gpost_ref[...], gate_ref[...])


def _ffn_call(x, layer, mod4, norm4, wup, cw, cb, wdn):
    b, s, d = x.shape
    hidden = wdn.shape[1]
    tm = FFN_ROW_TILE
    tile = lambda bb, i: (bb, i, 0)
    at_layer = lambda bb, i: (layer, 0, 0)
    return pl.pallas_call(
        functools.partial(_ffn_kernel, tn=FFN_COL_TILE),
        grid=(b, s // tm),
        in_specs=[pl.BlockSpec((1, tm, d), tile)] + _halo_specs(tm, d, s) + [
            _row_spec(d, lambda bb, i: (layer, bb, 0, 3)),
            _row_spec(d, lambda bb, i: (layer, bb, 0, 4)),
            _row_spec(d, lambda bb, i: (layer, bb, 0, 5)),
            _row_spec(d, lambda bb, i: (layer, 2, 0, 0)),
            _row_spec(d, lambda bb, i: (layer, 3, 0, 0)),
            pl.BlockSpec((None, d, 2 * hidden), at_layer, pipeline_mode=pl.Buffered(1)),
            pl.BlockSpec((None,) + cw.shape[1:], at_layer),
            pl.BlockSpec((None, 1, hidden), at_layer),
            pl.BlockSpec((None, hidden, d), at_layer, pipeline_mode=pl.Buffered(1)),
        ],
        out_specs=pl.BlockSpec((1, tm, d), tile),
        out_shape=jax.ShapeDtypeStruct((b, s, d), F32),
        scratch_shapes=[pltpu.VMEM((tm + 2 * HALO, d), BF16)],
        compiler_params=_cparams(2),
        name="conv_ffn",
    )(x, x, x, mod4, mod4, mod4, norm4, norm4, wup, cw, cb, wdn)


def _window_sum(h_ext, w, rows):
    n = h_ext.shape[0]
    acc = h_ext
    span = 1
    while span < w:
        acc = acc + pltpu.roll(acc, span, 0)
        span *= 2
    shift = w // 2 - 1
    if shift:
        acc = pltpu.roll(acc, n - shift, 0)
    return acc[HALO:HALO + rows]


def _pool_kernel(x_ref, prev_ref, next_ref, sh_ref, sc_ref, gate_ref, gpre_ref, gpost_ref,
                 pw_ref, pb_ref, ps_ref, o_ref, h_scr, *, seq):
    i = pl.program_id(1)
    nt = pl.num_programs(1)
    ts = x_ref.shape[1]
    x = x_ref[0]
    gwidth = pw_ref.shape[1]

    def nm(xt):
        return _norm_mod(xt, gpre_ref[...], sc_ref[...], sh_ref[...])

    h_scr[0:HALO, :] = jnp.where(i > 0, nm(prev_ref[0]), 0.0)
    h_scr[HALO:HALO + ts, :] = nm(x)
    h_scr[HALO + ts:, :] = jnp.where(i < nt - 1, nm(next_ref[0]), 0.0)
    t = i * ts + lax.broadcasted_iota(jnp.int32, (ts, 1), 0)
    ys = []
    for gi, w in enumerate(POOL_WINDOWS):
        cols = slice(gi * gwidth, (gi + 1) * gwidth)
        h_ext = h_scr[:, cols]
        cnt = (jnp.minimum(t + w // 2, seq) - jnp.maximum(t - w // 2, 0)).astype(F32)
        pooled = _window_sum(h_ext, w, ts) / cnt - h_ext[HALO:HALO + ts]
        ys.append(_dot(pooled.astype(BF16), pw_ref[gi]) + pb_ref[:, cols])
    y = jnp.concatenate(ys, axis=-1) * ps_ref[...]
    o_ref[0] = _gated_norm_add(x, y, gpost_ref[...], gate_ref[...])


def _pool_call(x, layer, mod4, norm4, pw, pb, ps):
    b, s, d = x.shape
    ts = ROW_TILE
    tile = lambda bb, i: (bb, i, 0)
    fixed = lambda bb, i: (0, 0)
    return pl.pallas_call(
        functools.partial(_pool_kernel, seq=s),
        grid=(b, s // ts),
        in_specs=[pl.BlockSpec((1, ts, d), tile)] + _halo_specs(ts, d, s) + [
            _row_spec(d, lambda bb, i: (layer, bb, 0, 0)),
            _row_spec(d, lambda bb, i: (layer, bb, 0, 1)),
            _row_spec(d, lambda bb, i: (layer, bb, 0, 2)),
            _row_spec(d, lambda bb, i: (layer, 0, 0, 0)),
            _row_spec(d, lambda bb, i: (layer, 1, 0, 0)),
            pl.BlockSpec(pw.shape, lambda bb, i: (0, 0, 0)),
            pl.BlockSpec((1, d), fixed),
            pl.BlockSpec((1, d), fixed),
        ],
        out_specs=pl.BlockSpec((1, ts, d), tile),
        out_shape=jax.ShapeDtypeStruct((b, s, d), F32),
        scratch_shapes=[pltpu.VMEM((ts + 2 * HALO, d), F32)],
        compiler_params=_cparams(2),
        name="pool_mixer",
    )(x, x, x, mod4, mod4, mod4, norm4, norm4, pw, pb, ps)


def kernel(x, c, ctx, c_ctx, ada_w, ada_b, norm_g, w_in, w_out, na_rpb, ssm_conv_w, ssm_conv_b, ssm_a_log, ssm_dt_bias, ssm_d, ssm_norm_g, pool_w, pool_b, pool_scale, ffn_w_up, ffn_conv_w, ffn_conv_b, ffn_w_down):
    b, s, d = x.shape
    l = ctx.shape[1]
    depth = ada_w.shape[0]
    hidden = ffn_w_down.shape[1]
    rows = s // GRID_W
    assert depth == 2 and l % (SSD_STEP_CHUNKS * SSD_CHUNK) == 0 and s % l == 0 and s % (NA_STEP_BLOCKS * NA_QROWS * GRID_W) == 0
    assert s % FFN_ROW_TILE == 0 and s % ROW_TILE == 0 and hidden % FFN_COL_TILE == 0
    assert rows >= NA_KROWS and 2 * SSM_HEADS <= LANES
    assert NA_DOFF + 2 * NA_WIN_ROWS - 1 <= NA_NTILES + 1 and NA_KROWS + NA_QROWS + NA_DOFF <= NA_NTILES + 1

    rows_c = -(-(b + 1) // 8) * 8
    c_ext = jnp.zeros((rows_c, d), F32).at[:b].set(c).at[b].set(c_ctx)
    mod4 = _ada_call(c_ext, ada_w, ada_b).reshape(depth, rows_c, 1, 6 * d)
    norm4 = norm_g.reshape(depth, 4, 1, d)
    wup = ffn_w_up.astype(BF16)
    wdn = ffn_w_down.astype(BF16)
    ffn_cb = ffn_conv_b.reshape(depth, 1, hidden)

    w_cat = jnp.pad(w_in, ((0, 0), (0, 0), (0, LANES - 2 * SSM_HEADS))).astype(BF16)
    (q, z, k, v, xbc, dt_raw), (kc, vc, xbc_c, dt_c) = _inproj_call(
        x, ctx, mod4, norm4, w_cat, ssm_conv_w, ssm_conv_b.reshape(1, 1, SSM_CONV_DIM))
    att = _na_call(q, k, v, kc, vc, _na_bias_tiles(na_rpb[0]))
    pad = LANES - 2 * SSM_HEADS
    prm = jnp.stack([jnp.pad(ssm_dt_bias[0].reshape(-1), (0, pad)), jnp.pad(ssm_a_log[0].reshape(-1), (0, pad))])
    dsk = jnp.repeat(ssm_d[0], SSM_HEAD_DIM, axis=1)
    y_f, y_b = _ssd_call(xbc, dt_raw, xbc_c, dt_c, prm, dsk)
    x = _outproj_call(x, att, y_f, y_b, z, mod4, norm4, ssm_norm_g.reshape(1, SSM_INNER), w_out.astype(BF16))
    x = _ffn_call(x, 0, mod4, norm4, wup, ffn_conv_w, ffn_cb, wdn)

    x = _pool_call(x, 1, mod4, norm4, pool_w[0].astype(BF16), pool_b.reshape(1, d), pool_scale.reshape(1, d))
    x = _ffn_call(x, 1, mod4, norm4, wup, ffn_conv_w, ffn_cb, wdn)
    return x
```

```python
import functools

import numpy as np
import jax
import jax.numpy as jnp
from jax import lax
from jax.experimental import pallas as pl
from jax.experimental.pallas import tpu as pltpu

F32 = jnp.float32
BF16 = jnp.bfloat16

GRID_W = 64
NA_HEADS = 8
NA_HEAD_DIM = 64
NA_WIDTH = NA_HEADS * NA_HEAD_DIM
NA_WIN_ROWS = 8
NA_WIN_COLS = 16
SSM_HEADS = 16
SSM_HEAD_DIM = 64
SSM_INNER = SSM_HEADS * SSM_HEAD_DIM
SSM_GROUPS = 4
SSM_STATE = 128
SSD_CHUNK = 128
SSD_STEP_CHUNKS = 2
SSM_BC = SSM_GROUPS * SSM_STATE
SSM_CONV_DIM = SSM_INNER + 2 * SSM_BC
POOL_WINDOWS = (2, 4, 8, 16)
RMS_EPS = 1e-6

LANES = 128
HALO = 16
NA_QROWS = 8
NA_KROWS = 16
NA_STEP_BLOCKS = 4
NA_ROW_GROUPS = 2
NA_DOFF = NA_QROWS
NA_NTILES = 32
NEG = -1e30
VMEM_LIMIT = 56 * 1024 * 1024
ROW_TILE = 1024
FFN_ROW_TILE = 1024
FFN_COL_TILE = 256


def _cparams(n_axes):
    return pltpu.CompilerParams(dimension_semantics=("arbitrary",) * n_axes, vmem_limit_bytes=VMEM_LIMIT)


def _rms_scale(x):
    return lax.rsqrt(jnp.mean(x * x, axis=-1, keepdims=True) + RMS_EPS)


def _norm_mod(x, g, sc, sh):
    return x * _rms_scale(x) * (g * (1.0 + sc)) + sh


def _gated_norm_add(x, y, g, gate):
    return x + y * _rms_scale(y) * (g * gate)


def _silu(x):
    return x * jax.nn.sigmoid(x)


def _dot(a, b):
    return jnp.dot(a, b, preferred_element_type=F32)


def _dot_nt(a, b):
    return lax.dot_general(a, b, (((1,), (1,)), ((), ())), preferred_element_type=F32)


def _split3(a):
    hi = a.astype(BF16)
    r = a - hi.astype(F32)
    mid = r.astype(BF16)
    lo = (r - mid.astype(F32)).astype(BF16)
    return hi, mid, lo


def _conv3_rows(u, w_rows, rows):
    n = u.shape[0]
    u_prev = pltpu.roll(u, 1, 0)[HALO:HALO + rows]
    u_next = pltpu.roll(u, n - 1, 0)[HALO:HALO + rows]
    return u_prev * w_rows[0:1] + u[HALO:HALO + rows] * w_rows[1:2] + u_next * w_rows[2:3]


def _row_spec(d, index):
    return pl.BlockSpec((None, None, 1, d), index)


def _halo_specs(tm, d, s):
    rh = tm // HALO
    return [
        pl.BlockSpec((1, HALO, d), lambda bb, i: (bb, jnp.maximum(i * rh - 1, 0), 0)),
        pl.BlockSpec((1, HALO, d), lambda bb, i: (bb, jnp.minimum((i + 1) * rh, s // HALO - 1), 0)),
    ]


def _ada_kernel(c_ref, w_ref, b_ref, o_ref):
    s = _silu(c_ref[...])
    w = w_ref[0]
    acc = jnp.zeros(o_ref.shape[1:], F32)
    s_parts = _split3(s)[:2]
    w_parts = _split3(w)[:2]
    for si, sp in enumerate(s_parts):
        for wi, wp in enumerate(w_parts):
            if si + wi <= 1:
                acc = acc + _dot(sp, wp)
    o_ref[0] = acc + b_ref[0]


def _ada_call(c_ext, ada_w, ada_b):
    depth, d, n = ada_w.shape
    r = c_ext.shape[0]
    tn = 768
    return pl.pallas_call(
        _ada_kernel,
        grid=(depth, n // tn),
        in_specs=[
            pl.BlockSpec((r, d), lambda l, j: (0, 0)),
            pl.BlockSpec((1, d, tn), lambda l, j: (l, 0, j)),
            pl.BlockSpec((1, 1, tn), lambda l, j: (l, 0, j)),
        ],
        out_specs=pl.BlockSpec((1, r, tn), lambda l, j: (l, 0, j)),
        out_shape=jax.ShapeDtypeStruct((depth, r, n), F32),
        compiler_params=_cparams(2),
        name="ada_mod",
    )(c_ext, ada_w, ada_b.reshape(depth, 1, n))


def _inproj_body(h_scr, w_ref, cw_ref, cb_ref, qz_refs, k_ref, v_ref, xbc_ref, dt_ref, tm):
    hb = h_scr[HALO:HALO + tm, :]
    if qz_refs is not None:
        q_ref, z_ref = qz_refs
        q_ref[0] = _dot(hb, w_ref[:, 0:NA_WIDTH]).astype(q_ref.dtype)
        z_ref[0] = _dot(hb, w_ref[:, NA_WIDTH:NA_WIDTH + SSM_INNER]).astype(z_ref.dtype)
    o = NA_WIDTH + SSM_INNER
    k_ref[0] = _dot(hb, w_ref[:, o:o + NA_WIDTH]).astype(k_ref.dtype)
    o += NA_WIDTH
    v_ref[0] = _dot(hb, w_ref[:, o:o + NA_WIDTH]).astype(v_ref.dtype)
    o += NA_WIDTH
    cblk = 512
    for c0 in range(0, SSM_CONV_DIM, cblk):
        u = _dot(h_scr[...], w_ref[:, o + c0:o + c0 + cblk])
        y = _conv3_rows(u, cw_ref[:, c0:c0 + cblk], tm) + cb_ref[:, c0:c0 + cblk]
        xbc_ref[0, :, c0:c0 + cblk] = _silu(y).astype(xbc_ref.dtype)
    o += SSM_CONV_DIM
    dt_ref[0] = _dot(hb, w_ref[:, o:o + LANES]).astype(dt_ref.dtype)


def _inproj_lat_kernel(x_ref, prev_ref, next_ref, sh_ref, sc_ref, g_ref, w_ref, cw_ref, cb_ref,
                       q_ref, z_ref, k_ref, v_ref, xbc_ref, dt_ref, h_scr):
    i = pl.program_id(1)
    nt = pl.num_programs(1)
    tm = x_ref.shape[1]

    def nm(xt):
        return _norm_mod(xt, g_ref[...], sc_ref[...], sh_ref[...]).astype(BF16)

    zero = jnp.zeros((), BF16)
    h_scr[0:HALO, :] = jnp.where(i > 0, nm(prev_ref[0]), zero)
    h_scr[HALO:HALO + tm, :] = nm(x_ref[0])
    h_scr[HALO + tm:, :] = jnp.where(i < nt - 1, nm(next_ref[0]), zero)
    _inproj_body(h_scr, w_ref, cw_ref, cb_ref, (q_ref, z_ref), k_ref, v_ref, xbc_ref, dt_ref, tm)


def _inproj_ctx_kernel(x_ref, sh_ref, sc_ref, g_ref, w_ref, cw_ref, cb_ref, k_ref, v_ref, xbc_ref, dt_ref, h_scr):
    tm = x_ref.shape[1]
    h_scr[0:HALO, :] = jnp.zeros((HALO, h_scr.shape[1]), BF16)
    h_scr[HALO:HALO + tm, :] = _norm_mod(x_ref[0], g_ref[...], sc_ref[...], sh_ref[...]).astype(BF16)
    h_scr[HALO + tm:, :] = jnp.zeros((HALO, h_scr.shape[1]), BF16)
    _inproj_body(h_scr, w_ref, cw_ref, cb_ref, None, k_ref, v_ref, xbc_ref, dt_ref, tm)


def _inproj_call(x, ctx, mod4, norm4, w, cw, cb):
    b, s, d = x.shape
    l = ctx.shape[1]
    tm = ROW_TILE
    n = w.shape[-1]
    tile = lambda bb, i: (bb, i, 0)
    fixed = lambda bb, i: (0, 0, 0)
    weights = [
        pl.BlockSpec((None, d, n), fixed, pipeline_mode=pl.Buffered(1)),
        pl.BlockSpec((None,) + cw.shape[1:], fixed),
        pl.BlockSpec((None,) + cb.shape[1:], fixed),
    ]
    widths = (NA_WIDTH, NA_WIDTH, SSM_CONV_DIM, LANES)
    dtypes = (BF16, BF16, BF16, F32)
    q, z, k, v, xbc, dt = pl.pallas_call(
        _inproj_lat_kernel,
        grid=(b, s // tm),
        in_specs=[pl.BlockSpec((1, tm, d), tile)] + _halo_specs(tm, d, s) + [
            _row_spec(d, lambda bb, i: (0, bb, 0, 0)),
            _row_spec(d, lambda bb, i: (0, bb, 0, 1)),
            _row_spec(d, lambda bb, i: (0, 0, 0, 0)),
        ] + weights,
        out_specs=[pl.BlockSpec((1, tm, NA_WIDTH), tile), pl.BlockSpec((1, tm, SSM_INNER), tile)]
        + [pl.BlockSpec((1, tm, wd), tile) for wd in widths],
        out_shape=[jax.ShapeDtypeStruct((b, s, NA_WIDTH), BF16), jax.ShapeDtypeStruct((b, s, SSM_INNER), BF16)]
        + [jax.ShapeDtypeStruct((b, s, wd), dt_) for wd, dt_ in zip(widths, dtypes)],
        scratch_shapes=[pltpu.VMEM((tm + 2 * HALO, d), BF16)],
        compiler_params=_cparams(2),
        name="inproj",
    )(x, x, x, mod4, mod4, norm4, w, cw, cb)

    kc, vc, xbc_c, dt_c = pl.pallas_call(
        _inproj_ctx_kernel,
        grid=(b, 1),
        in_specs=[
            pl.BlockSpec((1, l, d), tile),
            _row_spec(d, lambda bb, i: (0, b, 0, 0)),
            _row_spec(d, lambda bb, i: (0, b, 0, 1)),
            _row_spec(d, lambda bb, i: (0, 0, 0, 0)),
        ] + weights,
        out_specs=[pl.BlockSpec((1, l, wd), tile) for wd in widths],
        out_shape=[jax.ShapeDtypeStruct((b, l, wd), dt_) for wd, dt_ in zip(widths, dtypes)],
        scratch_shapes=[pltpu.VMEM((l + 2 * HALO, d), BF16)],
        compiler_params=_cparams(2),
        name="inproj_ctx",
    )(ctx, mod4, mod4, norm4, w, cw, cb)
    return (q, z, k, v, xbc, dt), (kc, vc, xbc_c, dt_c)


def _na_bias_tiles(rpb):
    h, n_rel_r, n_rel_c = rpb.shape
    w = GRID_W
    c = np.arange(w)[:, None]
    kc = np.arange(w)[None, :]
    cs = np.clip(c - NA_WIN_COLS // 2, 0, w - NA_WIN_COLS)
    col_ok = (kc >= cs) & (kc < cs + NA_WIN_COLS)
    pick = ((kc - c + NA_WIN_COLS - 1)[None] == np.arange(n_rel_c)[:, None, None]) & col_ok[None]
    t = jnp.einsum("hrd,dck->hrck", rpb, jnp.asarray(pick, F32), precision=lax.Precision.HIGHEST)
    t = jnp.where(jnp.asarray(col_ok)[None, None], t, NEG)
    n_after = NA_NTILES + 1 - NA_DOFF - n_rel_r
    tz = jnp.concatenate(
        [jnp.full((h, NA_DOFF, w, w), NEG, F32), t, jnp.full((h, n_after, w, w), NEG, F32)], axis=1)
    return jnp.concatenate([tz[:, :-1], tz[:, 1:]], axis=-1)


def _na_window_rows(kind, rows):
    r0 = {"first": 0, "interior": NA_QROWS, "last": rows - NA_QROWS}[kind]
    k0 = min(max(r0 - NA_WIN_ROWS // 2, 0), rows - NA_KROWS)
    los = [min(max(r0 + i - NA_WIN_ROWS // 2, 0), rows - NA_WIN_ROWS) - k0 for i in range(NA_QROWS)]
    return k0 - r0, los


def _na_block(q_ref, k_ref, v_ref, kc_ref, vc_ref, pt_ref, o_ref, sub, kind, rows):
    nk = NA_KROWS * GRID_W
    nq = NA_QROWS * GRID_W
    w = GRID_W
    delta, los = _na_window_rows(kind, rows)
    r0 = (pl.program_id(2) * NA_STEP_BLOCKS + sub) * NA_QROWS
    kstart = pl.multiple_of((r0 + delta) * GRID_W, GRID_W)
    q = q_ref[0, sub * nq:(sub + 1) * nq, :]
    kw = k_ref[0, pl.ds(kstart, nk), :]
    vw = v_ref[0, pl.ds(kstart, nk), :]
    kc = kc_ref[0]
    vc = vc_ref[0]
    lane = lax.broadcasted_iota(jnp.int32, (1, LANES), 1)
    one = jnp.ones((), v_ref.dtype)
    scale = jnp.asarray(NA_HEAD_DIM ** -0.5, q_ref.dtype)
    pairs = [(lo // 2, (lo + NA_WIN_ROWS - 1) // 2) for lo in los]
    gsz = NA_QROWS // NA_ROW_GROUPS
    out = jnp.zeros((nq, LANES), F32)
    for hh in range(LANES // NA_HEAD_DIM):
        sel = jnp.logical_and(lane >= hh * NA_HEAD_DIM, lane < (hh + 1) * NA_HEAD_DIM)
        qm = jnp.where(sel, q, jnp.zeros_like(q)) * scale
        vw_h = jnp.where(sel, vw, one)
        vc_h = jnp.where(sel, vc, one)
        o_groups = []
        for grp in range(NA_ROW_GROUPS):
            irange = range(grp * gsz, (grp + 1) * gsz)
            p_lo = min(pairs[i][0] for i in irange)
            p_hi = max(pairs[i][1] for i in irange)
            qg = qm[grp * gsz * w:(grp + 1) * gsz * w, :]
            s_raw = _dot_nt(qg, kw[p_lo * 2 * w:(p_hi + 1) * 2 * w, :])
            sc = _dot_nt(qg, kc)
            p_rows, pc_rows = [], []
            for i in irange:
                lo = los[i]
                jj_lo, jj_hi = pairs[i]
                ri = i - grp * gsz
                blocks = []
                for jj in range(jj_lo, jj_hi + 1):
                    t_idx = delta + 2 * jj - i + (NA_WIN_ROWS - 1) + NA_DOFF
                    blk = s_raw[ri * w:(ri + 1) * w, (jj - p_lo) * 2 * w:(jj - p_lo + 1) * 2 * w] + pt_ref[hh, t_idx]
                    left_out, right_out = 2 * jj < lo, 2 * jj + 1 >= lo + NA_WIN_ROWS
                    if left_out or right_out:
                        blk = blk + jnp.where(lane < w, NEG if left_out else 0.0, NEG if right_out else 0.0)
                    blocks.append(blk)
                s_i = jnp.concatenate(blocks, axis=1)
                sc_i = sc[ri * w:(ri + 1) * w, :]
                m = jnp.maximum(jnp.max(s_i, axis=-1, keepdims=True), jnp.max(sc_i, axis=-1, keepdims=True))
                pieces = [jnp.exp(s_i - m).astype(BF16)]
                if jj_lo > p_lo:
                    pieces.insert(0, jnp.zeros((w, (jj_lo - p_lo) * 2 * w), BF16))
                if jj_hi < p_hi:
                    pieces.append(jnp.zeros((w, (p_hi - jj_hi) * 2 * w), BF16))
                p_rows.append(jnp.concatenate(pieces, axis=1) if len(pieces) > 1 else pieces[0])
                pc_rows.append(jnp.exp(sc_i - m).astype(BF16))
            p = jnp.concatenate(p_rows, axis=0)
            pc = jnp.concatenate(pc_rows, axis=0)
            o_groups.append(_dot(p, vw_h[p_lo * 2 * w:(p_hi + 1) * 2 * w, :]) + _dot(pc, vc_h))
        o = jnp.concatenate(o_groups, axis=0)
        den = pltpu.roll(o, NA_HEAD_DIM, 1)
        out = jnp.where(sel, o / den, out)
    o_ref[0, sub * nq:(sub + 1) * nq, :] = out.astype(o_ref.dtype)


def _na_kernel(q_ref, k_ref, v_ref, kc_ref, vc_ref, pt_ref, o_ref, *, rows):
    step = pl.program_id(2)
    nsteps = rows // (NA_QROWS * NA_STEP_BLOCKS)

    def run(kinds):
        for sub, kind in enumerate(kinds):
            _na_block(q_ref, k_ref, v_ref, kc_ref, vc_ref, pt_ref, o_ref, sub, kind, rows)

    inner = ("interior",) * (NA_STEP_BLOCKS - 1)
    if nsteps == 1:
        run(("first",) + ("interior",) * (NA_STEP_BLOCKS - 2) + ("last",))
    else:
        pl.when(step == 0)(lambda: run(("first",) + inner))
        pl.when(jnp.logical_and(step > 0, step < nsteps - 1))(lambda: run(inner + ("interior",)))
        pl.when(step == nsteps - 1)(lambda: run(inner + ("last",)))


def _na_call(q, k, v, kc, vc, pt):
    b, s, _ = q.shape
    l = kc.shape[1]
    rows = s // GRID_W
    nq = NA_STEP_BLOCKS * NA_QROWS * GRID_W
    hp = LANES // NA_HEAD_DIM
    per_batch = lambda p, bb, rb: (bb, 0, p)
    return pl.pallas_call(
        functools.partial(_na_kernel, rows=rows),
        grid=(NA_WIDTH // LANES, b, s // nq),
        in_specs=[
            pl.BlockSpec((1, nq, LANES), lambda p, bb, rb: (bb, rb, p)),
            pl.BlockSpec((1, s, LANES), per_batch),
            pl.BlockSpec((1, s, LANES), per_batch),
            pl.BlockSpec((1, l, LANES), per_batch),
            pl.BlockSpec((1, l, LANES), per_batch),
            pl.BlockSpec((hp,) + pt.shape[1:], lambda p, bb, rb: (p, 0, 0, 0)),
        ],
        out_specs=pl.BlockSpec((1, nq, LANES), lambda p, bb, rb: (bb, rb, p)),
        out_shape=jax.ShapeDtypeStruct((b, s, NA_WIDTH), BF16),
        compiler_params=_cparams(3),
        name="nbr_attention",
    )(q, k, v, kc, vc, pt)


def _ssd_decays(dt_ref, prm_ref, row0, *, reverse):
    q = SSD_CHUNK
    dt = jax.nn.softplus(dt_ref[0, row0:row0 + q, :] + prm_ref[0:1, :])
    a = -dt * jnp.exp(prm_ref[1:2, :])
    ri = lax.broadcasted_iota(jnp.int32, (q, q), 0)
    ci = lax.broadcasted_iota(jnp.int32, (q, q), 1)
    tri = (ci >= ri) if reverse else (ci <= ri)
    ones = jnp.where(tri, 1.0, 0.0).astype(BF16)
    cs = jnp.zeros((q, LANES), F32)
    for part in _split3(a):
        cs = cs + _dot(ones, part)
    tot_row = cs[q - 1:q, :] if not reverse else cs[0:1, :]
    cs_t = cs.T
    dt_t = dt.T
    e_end_t = (jnp.exp(tot_row.T - cs_t) * dt_t).astype(BF16)
    seg_row = cs_t - jnp.log(dt_t)
    e_start = jnp.exp(cs)
    dec = jnp.exp(tot_row)
    return tri, cs, seg_row, e_end_t, e_start, dec


def _ssd_direction(xbc_ref, row0, decays, dsk_ref, state_ref, y_ref, *, reverse, emit_y):
    q = SSD_CHUNK
    tri, cs, seg_row, e_end_t, e_start, dec = decays
    lane_off = SSM_HEADS if reverse else 0
    lane = lax.broadcasted_iota(jnp.int32, (1, LANES), 1)
    lo = lane < SSM_HEAD_DIM
    rows = slice(row0, row0 + q)
    for g in range(SSM_GROUPS):
        bm = xbc_ref[0, rows, SSM_INNER + g * SSM_STATE:SSM_INNER + (g + 1) * SSM_STATE]
        cm = xbc_ref[0, rows, SSM_INNER + SSM_BC + g * SSM_STATE:SSM_INNER + SSM_BC + (g + 1) * SSM_STATE]
        bt = bm.astype(F32).T.astype(BF16)
        if emit_y:
            cb = _dot_nt(cm, bm).astype(BF16)
            cmf = cm.astype(F32)
        heads_per_group = SSM_HEADS // SSM_GROUPS
        for pp in range(heads_per_group // 2):
            pair = g * (heads_per_group // 2) + pp
            xs = xbc_ref[0, rows, pair * LANES:(pair + 1) * LANES]
            xs_lo = jnp.where(lo, xs, jnp.zeros_like(xs))
            xs_hi = jnp.where(lo, jnp.zeros_like(xs), xs)
            rhs_x = jnp.concatenate([xs_lo, xs_hi], axis=0)
            st = state_ref[pair]
            h0 = lane_off + 2 * pair
            btw = [bt * e_end_t[h0 + u:h0 + u + 1, :] for u in range(2)]
            upd = _dot(jnp.concatenate(btw, axis=1), rhs_x)
            if emit_y:
                stb = st.astype(BF16)
                st_lo = jnp.where(lo, stb, jnp.zeros_like(stb))
                st_hi = jnp.where(lo, jnp.zeros_like(stb), stb)
                lhs = []
                for u in range(2):
                    hcol = cs[:, h0 + u:h0 + u + 1]
                    seg = jnp.exp(jnp.where(tri, hcol - seg_row[h0 + u:h0 + u + 1, :], NEG))
                    lhs.append(cb * seg.astype(BF16))
                for u in range(2):
                    lhs.append((cmf * e_start[:, h0 + u:h0 + u + 1]).astype(BF16))
                y = _dot(jnp.concatenate(lhs, axis=1), jnp.concatenate([rhs_x, st_lo, st_hi], axis=0))
                y = y + dsk_ref[:, pair * LANES:(pair + 1) * LANES] * xs.astype(F32)
                y_ref[0, rows, pair * LANES:(pair + 1) * LANES] = y.astype(y_ref.dtype)
            dpair = jnp.where(lo, dec[:, h0:h0 + 1], dec[:, h0 + 1:h0 + 2])
            state_ref[pair] = st * dpair + upd


def _ssd_kernel(xf_ref, dtf_ref, xb_ref, dtb_ref, xc_ref, dtc_ref, prm_ref, dsk_ref, yf_ref, yb_ref,
                sf_ref, sb_ref, *, n_ctx):
    s = pl.program_id(1)
    step = SSD_STEP_CHUNKS * SSD_CHUNK

    @pl.when(s == 0)
    def _():
        sf_ref[...] = jnp.zeros_like(sf_ref)
        sb_ref[...] = jnp.zeros_like(sb_ref)

    def run(x_f, dt_f, x_b, dt_b, base_f, base_b, emit_y):
        order_f = [c * SSD_CHUNK for c in range(SSD_STEP_CHUNKS)]
        order_b = order_f[::-1]
        dec_f = [_ssd_decays(dt_f, prm_ref, base_f + r, reverse=False) for r in order_f]
        dec_b = [_ssd_decays(dt_b, prm_ref, base_b + r, reverse=True) for r in order_b]
        for c in range(SSD_STEP_CHUNKS):
            _ssd_direction(x_f, base_f + order_f[c], dec_f[c], dsk_ref.at[0:1], sf_ref, yf_ref,
                           reverse=False, emit_y=emit_y)
            _ssd_direction(x_b, base_b + order_b[c], dec_b[c], dsk_ref.at[1:2], sb_ref, yb_ref,
                           reverse=True, emit_y=emit_y)

    for j in range(n_ctx):
        pl.when(s == j)(functools.partial(run, xc_ref, dtc_ref, xc_ref, dtc_ref,
                                          j * step, (n_ctx - 1 - j) * step, False))

    @pl.when(s >= n_ctx)
    def _():
        run(xf_ref, dtf_ref, xb_ref, dtb_ref, 0, 0, True)


def _ssd_call(xbc, dt, xbc_c, dt_c, prm, dsk):
    b, n_lat, c = xbc.shape
    l = xbc_c.shape[1]
    q = SSD_STEP_CHUNKS * SSD_CHUNK
    nl = n_lat // q
    n_ctx = l // q
    fwd = lambda bb, s: (bb, jnp.maximum(s - n_ctx, 0), 0)
    bwd = lambda bb, s: (bb, jnp.minimum(nl - 1 - (s - n_ctx), nl - 1), 0)
    whole = lambda bb, s: (bb, 0, 0)
    fixed = lambda bb, s: (0, 0)
    return pl.pallas_call(
        functools.partial(_ssd_kernel, n_ctx=n_ctx),
        grid=(b, n_ctx + nl),
        in_specs=[
            pl.BlockSpec((1, q, c), fwd),
            pl.BlockSpec((1, q, LANES), fwd),
            pl.BlockSpec((1, q, c), bwd),
            pl.BlockSpec((1, q, LANES), bwd),
            pl.BlockSpec((1, l, c), whole),
            pl.BlockSpec((1, l, LANES), whole),
            pl.BlockSpec((2, LANES), fixed),
            pl.BlockSpec((2, SSM_INNER), fixed),
        ],
        out_specs=[
            pl.BlockSpec((1, q, SSM_INNER), fwd),
            pl.BlockSpec((1, q, SSM_INNER), bwd),
        ],
        out_shape=[jax.ShapeDtypeStruct((b, n_lat, SSM_INNER), BF16)] * 2,
        scratch_shapes=[pltpu.VMEM((SSM_HEADS // 2, SSM_STATE, LANES), F32)] * 2,
        compiler_params=_cparams(2),
        name="ssd_scan",
    )(xbc, dt, xbc, dt, xbc_c, dt_c, prm, dsk)


def _outproj_kernel(x_ref, att_ref, yf_ref, yb_ref, z_ref, gate_ref, ng_ref, gpost_ref, w_ref, o_ref):
    y = yf_ref[0].astype(F32) + yb_ref[0].astype(F32)
    yz = y * _silu(z_ref[0].astype(F32))
    gw = SSM_INNER // SSM_GROUPS
    parts = []
    for g in range(SSM_GROUPS):
        seg = yz[:, g * gw:(g + 1) * gw]
        parts.append(seg * _rms_scale(seg))
    ssm = (jnp.concatenate(parts, axis=-1) * ng_ref[...]).astype(BF16)
    o = _dot(att_ref[0], w_ref[0:NA_WIDTH, :]) + _dot(ssm, w_ref[NA_WIDTH:, :])
    o_ref[0] = _gated_norm_add(x_ref[0], o, gpost_ref[...], gate_ref[...])


def _outproj_call(x, att, yf, yb, z, mod4, norm4, ng, w):
    b, s, d = x.shape
    tm = ROW_TILE
    tile = lambda bb, i: (bb, i, 0)
    return pl.pallas_call(
        _outproj_kernel,
        grid=(b, s // tm),
        in_specs=[
            pl.BlockSpec((1, tm, d), tile),
            pl.BlockSpec((1, tm, NA_WIDTH), tile),
            pl.BlockSpec((1, tm, SSM_INNER), tile),
            pl.BlockSpec((1, tm, SSM_INNER), tile),
            pl.BlockSpec((1, tm, SSM_INNER), tile),
            _row_spec(d, lambda bb, i: (0, bb, 0, 2)),
            pl.BlockSpec((1, SSM_INNER), lambda bb, i: (0, 0)),
            _row_spec(d, lambda bb, i: (0, 1, 0, 0)),
            pl.BlockSpec((None,) + w.shape[1:], lambda bb, i: (0, 0, 0), pipeline_mode=pl.Buffered(1)),
        ],
        out_specs=pl.BlockSpec((1, tm, d), tile),
        out_shape=jax.ShapeDtypeStruct((b, s, d), F32),
        compiler_params=_cparams(2),
        name="outproj",
    )(x, att, yf, yb, z, mod4, ng, norm4, w)


def _ffn_kernel(x_ref, prev_ref, next_ref, sh_ref, sc_ref, gate_ref, gpre_ref, gpost_ref,
                wup_ref, cw_ref, cb_ref, wdn_ref, o_ref, h_scr, *, tn):
    i = pl.program_id(1)
    nt = pl.num_programs(1)
    tm = x_ref.shape[1]
    hidden = wdn_ref.shape[0]
    x = x_ref[0]

    def nm(xt):
        return _norm_mod(xt, gpre_ref[...], sc_ref[...], sh_ref[...]).astype(BF16)

    zero = jnp.zeros((), BF16)
    h_scr[0:HALO, :] = jnp.where(i > 0, nm(prev_ref[0]), zero)
    h_scr[HALO:HALO + tm, :] = nm(x)
    h_scr[HALO + tm:, :] = jnp.where(i < nt - 1, nm(next_ref[0]), zero)
    acc = jnp.zeros((tm, x.shape[1]), F32)
    for j in range(hidden // tn):
        u = _dot(h_scr[...], wup_ref[:, j * tn:(j + 1) * tn])
        v = _dot(h_scr[HALO:HALO + tm, :], wup_ref[:, hidden + j * tn:hidden + (j + 1) * tn])
        cv = _conv3_rows(u, cw_ref[:, j * tn:(j + 1) * tn], tm) + cb_ref[:, j * tn:(j + 1) * tn]
        gl = 0.5 * cv * (1.0 + lax.erf(cv * 0.7071067811865476)) * v
        acc = acc + _dot(gl.astype(BF16), wdn_ref[j * tn:(j + 1) * tn, :])
    o_ref[0] = _gated_norm_add(x, acc, gpost_ref[...], gate_ref[...])


def _ffn_call(x, layer, mod4, norm4, wup, cw, cb, wdn):
    b, s, d = x.shape
    hidden = wdn.shape[1]
    tm = FFN_ROW_TILE
    tile = lambda bb, i: (bb, i, 0)
    at_layer = lambda bb, i: (layer, 0, 0)
    return pl.pallas_call(
        functools.partial(_ffn_kernel, tn=FFN_COL_TILE),
        grid=(b, s // tm),
        in_specs=[pl.BlockSpec((1, tm, d), tile)] + _halo_specs(tm, d, s) + [
            _row_spec(d, lambda bb, i: (layer, bb, 0, 3)),
            _row_spec(d, lambda bb, i: (layer, bb, 0, 4)),
            _row_spec(d, lambda bb, i: (layer, bb, 0, 5)),
            _row_spec(d, lambda bb, i: (layer, 2, 0, 0)),
            _row_spec(d, lambda bb, i: (layer, 3, 0, 0)),
            pl.BlockSpec((None, d, 2 * hidden), at_layer, pipeline_mode=pl.Buffered(1)),
            pl.BlockSpec((None,) + cw.shape[1:], at_layer),
            pl.BlockSpec((None, 1, hidden), at_layer),
            pl.BlockSpec((None, hidden, d), at_layer, pipeline_mode=pl.Buffered(1)),
        ],
        out_specs=pl.BlockSpec((1, tm, d), tile),
        out_shape=jax.ShapeDtypeStruct((b, s, d), F32),
        scratch_shapes=[pltpu.VMEM((tm + 2 * HALO, d), BF16)],
        compiler_params=_cparams(2),
        name="conv_ffn",
    )(x, x, x, mod4, mod4, mod4, norm4, norm4, wup, cw, cb, wdn)


def _window_sum(h_ext, w, rows):
    n = h_ext.shape[0]
    acc = h_ext
    span = 1
    while span < w:
        acc = acc + pltpu.roll(acc, span, 0)
        span *= 2
    shift = w // 2 - 1
    if shift:
        acc = pltpu.roll(acc, n - shift, 0)
    return acc[HALO:HALO + rows]


def _pool_kernel(x_ref, prev_ref, next_ref, sh_ref, sc_ref, gate_ref, gpre_ref, gpost_ref,
                 pw_ref, pb_ref, ps_ref, o_ref, h_scr, *, seq):
    i = pl.program_id(1)
    nt = pl.num_programs(1)
    ts = x_ref.shape[1]
    x = x_ref[0]
    gwidth = pw_ref.shape[1]

    def nm(xt):
        return _norm_mod(xt, gpre_ref[...], sc_ref[...], sh_ref[...])

    h_scr[0:HALO, :] = jnp.where(i > 0, nm(prev_ref[0]), 0.0)
    h_scr[HALO:HALO + ts, :] = nm(x)
    h_scr[HALO + ts:, :] = jnp.where(i < nt - 1, nm(next_ref[0]), 0.0)
    t = i * ts + lax.broadcasted_iota(jnp.int32, (ts, 1), 0)
    ys = []
    for gi, w in enumerate(POOL_WINDOWS):
        cols = slice(gi * gwidth, (gi + 1) * gwidth)
        h_ext = h_scr[:, cols]
        cnt = (jnp.minimum(t + w // 2, seq) - jnp.maximum(t - w // 2, 0)).astype(F32)
        pooled = _window_sum(h_ext, w, ts) / cnt - h_ext[HALO:HALO + ts]
        ys.append(_dot(pooled.astype(BF16), pw_ref[gi]) + pb_ref[:, cols])
    y = jnp.concatenate(ys, axis=-1) * ps_ref[...]
    o_ref[0] = _gated_norm_add(x, y, gpost_ref[...], gate_ref[...])


def _pool_call(x, layer, mod4, norm4, pw, pb, ps):
    b, s, d = x.shape
    ts = ROW_TILE
    tile = lambda bb, i: (bb, i, 0)
    fixed = lambda bb, i: (0, 0)
    return pl.pallas_call(
        functools.partial(_pool_kernel, seq=s),
        grid=(b, s // ts),
        in_specs=[pl.BlockSpec((1, ts, d), tile)] + _halo_specs(ts, d, s) + [
            _row_spec(d, lambda bb, i: (layer, bb, 0, 0)),
            _row_spec(d, lambda bb, i: (layer, bb, 0, 1)),
            _row_spec(d, lambda bb, i: (layer, bb, 0, 2)),
            _row_spec(d, lambda bb, i: (layer, 0, 0, 0)),
            _row_spec(d, lambda bb, i: (layer, 1, 0, 0)),
            pl.BlockSpec(pw.shape, lambda bb, i: (0, 0, 0)),
            pl.BlockSpec((1, d), fixed),
            pl.BlockSpec((1, d), fixed),
        ],
        out_specs=pl.BlockSpec((1, ts, d), tile),
        out_shape=jax.ShapeDtypeStruct((b, s, d), F32),
        scratch_shapes=[pltpu.VMEM((ts + 2 * HALO, d), F32)],
        compiler_params=_cparams(2),
        name="pool_mixer",
    )(x, x, x, mod4, mod4, mod4, norm4, norm4, pw, pb, ps)


def kernel(x, c, ctx, c_ctx, ada_w, ada_b, norm_g, w_in, w_out, na_rpb, ssm_conv_w, ssm_conv_b, ssm_a_log, ssm_dt_bias, ssm_d, ssm_norm_g, pool_w, pool_b, pool_scale, ffn_w_up, ffn_conv_w, ffn_conv_b, ffn_w_down):
    b, s, d = x.shape
    l = ctx.shape[1]
    depth = ada_w.shape[0]
    hidden = ffn_w_down.shape[1]
    rows = s // GRID_W
    assert depth == 2 and l % (SSD_STEP_CHUNKS * SSD_CHUNK) == 0 and s % l == 0 and s % (NA_STEP_BLOCKS * NA_QROWS * GRID_W) == 0
    assert s % FFN_ROW_TILE == 0 and s % ROW_TILE == 0 and hidden % FFN_COL_TILE == 0
    assert rows >= NA_KROWS and 2 * SSM_HEADS <= LANES
    assert NA_DOFF + 2 * NA_WIN_ROWS - 1 <= NA_NTILES + 1 and NA_KROWS + NA_QROWS + NA_DOFF <= NA_NTILES + 1

    rows_c = -(-(b + 1) // 8) * 8
    c_ext = jnp.zeros((rows_c, d), F32).at[:b].set(c).at[b].set(c_ctx)
    mod4 = _ada_call(c_ext, ada_w, ada_b).reshape(depth, rows_c, 1, 6 * d)
    norm4 = norm_g.reshape(depth, 4, 1, d)
    wup = ffn_w_up.astype(BF16)
    wdn = ffn_w_down.astype(BF16)
    ffn_cb = ffn_conv_b.reshape(depth, 1, hidden)

    w_cat = jnp.pad(w_in, ((0, 0), (0, 0), (0, LANES - 2 * SSM_HEADS))).astype(BF16)
    (q, z, k, v, xbc, dt_raw), (kc, vc, xbc_c, dt_c) = _inproj_call(
        x, ctx, mod4, norm4, w_cat, ssm_conv_w, ssm_conv_b.reshape(1, 1, SSM_CONV_DIM))
    att = _na_call(q, k, v, kc, vc, _na_bias_tiles(na_rpb[0]))
    pad = LANES - 2 * SSM_HEADS
    prm = jnp.stack([jnp.pad(ssm_dt_bias[0].reshape(-1), (0, pad)), jnp.pad(ssm_a_log[0].reshape(-1), (0, pad))])
    dsk = jnp.repeat(ssm_d[0], SSM_HEAD_DIM, axis=1)
    y_f, y_b = _ssd_call(xbc, dt_raw, xbc_c, dt_c, prm, dsk)
    x = _outproj_call(x, att, y_f, y_b, z, mod4, norm4, ssm_norm_g.reshape(1, SSM_INNER), w_out.astype(BF16))
    x = _ffn_call(x, 0, mod4, norm4, wup, ffn_conv_w, ffn_cb, wdn)

    x = _pool_call(x, 1, mod4, norm4, pool_w[0].astype(BF16), pool_b.reshape(1, d), pool_scale.reshape(1, d))
    x = _ffn_call(x, 1, mod4, norm4, wup, ffn_conv_w, ffn_cb, wdn)
    return x
```

```python
import functools

import numpy as np
import jax
import jax.numpy as jnp
from jax import lax
from jax.experimental import pallas as pl
from jax.experimental.pallas import tpu as pltpu

F32 = jnp.float32
BF16 = jnp.bfloat16

GRID_W = 64
NA_HEADS = 8
NA_HEAD_DIM = 64
NA_WIDTH = NA_HEADS * NA_HEAD_DIM
NA_WIN_ROWS = 8
NA_WIN_COLS = 16
SSM_HEADS = 16
SSM_HEAD_DIM = 64
SSM_INNER = SSM_HEADS * SSM_HEAD_DIM
SSM_GROUPS = 4
SSM_STATE = 128
SSD_CHUNK = 128
SSD_STEP_CHUNKS = 2
SSM_BC = SSM_GROUPS * SSM_STATE
SSM_CONV_DIM = SSM_INNER + 2 * SSM_BC
POOL_WINDOWS = (2, 4, 8, 16)
RMS_EPS = 1e-6

LANES = 128
HALO = 16
NA_QROWS = 8
NA_KROWS = 16
NA_STEP_BLOCKS = 4
NA_ROW_GROUPS = 2
NA_DOFF = NA_QROWS
NA_NTILES = 32
NEG = -1e30
VMEM_LIMIT = 56 * 1024 * 1024
ROW_TILE = 1024
FFN_ROW_TILE = 1024
FFN_COL_TILE = 256


def _cparams(n_axes):
    return pltpu.CompilerParams(dimension_semantics=("arbitrary",) * n_axes, vmem_limit_bytes=VMEM_LIMIT)


def _rms_scale(x):
    return lax.rsqrt(jnp.mean(x * x, axis=-1, keepdims=True) + RMS_EPS)


def _norm_mod(x, g, sc, sh):
    return x * _rms_scale(x) * (g * (1.0 + sc)) + sh


def _gated_norm_add(x, y, g, gate):
    return x + y * _rms_scale(y) * (g * gate)


def _silu(x):
    return x * jax.nn.sigmoid(x)


def _dot(a, b):
    return jnp.dot(a, b, preferred_element_type=F32)


def _dot_nt(a, b):
    return lax.dot_general(a, b, (((1,), (1,)), ((), ())), preferred_element_type=F32)


def _split3(a):
    hi = a.astype(BF16)
    r = a - hi.astype(F32)
    mid = r.astype(BF16)
    lo = (r - mid.astype(F32)).astype(BF16)
    return hi, mid, lo


def _conv3_rows(u, w_rows, rows):
    n = u.shape[0]
    u_prev = pltpu.roll(u, 1, 0)[HALO:HALO + rows]
    u_next = pltpu.roll(u, n - 1, 0)[HALO:HALO + rows]
    return u_prev * w_rows[0:1] + u[HALO:HALO + rows] * w_rows[1:2] + u_next * w_rows[2:3]


def _row_spec(d, index):
    return pl.BlockSpec((None, None, 1, d), index)


def _halo_specs(tm, d, s):
    rh = tm // HALO
    return [
        pl.BlockSpec((1, HALO, d), lambda bb, i: (bb, jnp.maximum(i * rh - 1, 0), 0)),
        pl.BlockSpec((1, HALO, d), lambda bb, i: (bb, jnp.minimum((i + 1) * rh, s // HALO - 1), 0)),
    ]


def _ada_kernel(c_ref, w_ref, b_ref, o_ref):
    s = _silu(c_ref[...])
    w = w_ref[0]
    acc = jnp.zeros(o_ref.shape[1:], F32)
    s_parts = _split3(s)[:2]
    w_parts = _split3(w)[:2]
    for si, sp in enumerate(s_parts):
        for wi, wp in enumerate(w_parts):
            if si + wi <= 1:
                acc = acc + _dot(sp, wp)
    o_ref[0] = acc + b_ref[0]


def _ada_call(c_ext, ada_w, ada_b):
    depth, d, n = ada_w.shape
    r = c_ext.shape[0]
    tn = 768
    return pl.pallas_call(
        _ada_kernel,
        grid=(depth, n // tn),
        in_specs=[
            pl.BlockSpec((r, d), lambda l, j: (0, 0)),
            pl.BlockSpec((1, d, tn), lambda l, j: (l, 0, j)),
            pl.BlockSpec((1, 1, tn), lambda l, j: (l, 0, j)),
        ],
        out_specs=pl.BlockSpec((1, r, tn), lambda l, j: (l, 0, j)),
        out_shape=jax.ShapeDtypeStruct((depth, r, n), F32),
        compiler_params=_cparams(2),
        name="ada_mod",
    )(c_ext, ada_w, ada_b.reshape(depth, 1, n))


def _inproj_body(h_scr, w_ref, cw_ref, cb_ref, qz_refs, k_ref, v_ref, xbc_ref, dt_ref, tm):
    hb = h_scr[HALO:HALO + tm, :]
    if qz_refs is not None:
        q_ref, z_ref = qz_refs
        q_ref[0] = _dot(hb, w_ref[:, 0:NA_WIDTH]).astype(q_ref.dtype)
        z_ref[0] = _dot(hb, w_ref[:, NA_WIDTH:NA_WIDTH + SSM_INNER]).astype(z_ref.dtype)
    o = NA_WIDTH + SSM_INNER
    k_ref[0] = _dot(hb, w_ref[:, o:o + NA_WIDTH]).astype(k_ref.dtype)
    o += NA_WIDTH
    v_ref[0] = _dot(hb, w_ref[:, o:o + NA_WIDTH]).astype(v_ref.dtype)
    o += NA_WIDTH
    cblk = 512
    for c0 in range(0, SSM_CONV_DIM, cblk):
        u = _dot(h_scr[...], w_ref[:, o + c0:o + c0 + cblk])
        y = _conv3_rows(u, cw_ref[:, c0:c0 + cblk], tm) + cb_ref[:, c0:c0 + cblk]
        xbc_ref[0, :, c0:c0 + cblk] = _silu(y).astype(xbc_ref.dtype)
    o += SSM_CONV_DIM
    dt_ref[0] = _dot(hb, w_ref[:, o:o + LANES]).astype(dt_ref.dtype)


def _inproj_lat_kernel(x_ref, prev_ref, next_ref, sh_ref, sc_ref, g_ref, w_ref, cw_ref, cb_ref,
                       q_ref, z_ref, k_ref, v_ref, xbc_ref, dt_ref, h_scr):
    i = pl.program_id(1)
    nt = pl.num_programs(1)
    tm = x_ref.shape[1]

    def nm(xt):
        return _norm_mod(xt, g_ref[...], sc_ref[...], sh_ref[...]).astype(BF16)

    zero = jnp.zeros((), BF16)
    h_scr[0:HALO, :] = jnp.where(i > 0, nm(prev_ref[0]), zero)
    h_scr[HALO:HALO + tm, :] = nm(x_ref[0])
    h_scr[HALO + tm:, :] = jnp.where(i < nt - 1, nm(next_ref[0]), zero)
    _inproj_body(h_scr, w_ref, cw_ref, cb_ref, (q_ref, z_ref), k_ref, v_ref, xbc_ref, dt_ref, tm)


def _inproj_ctx_kernel(x_ref, sh_ref, sc_ref, g_ref, w_ref, cw_ref, cb_ref, k_ref, v_ref, xbc_ref, dt_ref, h_scr):
    tm = x_ref.shape[1]
    h_scr[0:HALO, :] = jnp.zeros((HALO, h_scr.shape[1]), BF16)
    h_scr[HALO:HALO + tm, :] = _norm_mod(x_ref[0], g_ref[...], sc_ref[...], sh_ref[...]).astype(BF16)
    h_scr[HALO + tm:, :] = jnp.zeros((HALO, h_scr.shape[1]), BF16)
    _inproj_body(h_scr, w_ref, cw_ref, cb_ref, None, k_ref, v_ref, xbc_ref, dt_ref, tm)


def _inproj_call(x, ctx, mod4, norm4, w, cw, cb):
    b, s, d = x.shape
    l = ctx.shape[1]
    tm = ROW_TILE
    n = w.shape[-1]
    tile = lambda bb, i: (bb, i, 0)
    fixed = lambda bb, i: (0, 0, 0)
    weights = [
        pl.BlockSpec((None, d, n), fixed, pipeline_mode=pl.Buffered(1)),
        pl.BlockSpec((None,) + cw.shape[1:], fixed),
        pl.BlockSpec((None,) + cb.shape[1:], fixed),
    ]
    widths = (NA_WIDTH, NA_WIDTH, SSM_CONV_DIM, LANES)
    dtypes = (BF16, BF16, BF16, F32)
    q, z, k, v, xbc, dt = pl.pallas_call(
        _inproj_lat_kernel,
        grid=(b, s // tm),
        in_specs=[pl.BlockSpec((1, tm, d), tile)] + _halo_specs(tm, d, s) + [
            _row_spec(d, lambda bb, i: (0, bb, 0, 0)),
            _row_spec(d, lambda bb, i: (0, bb, 0, 1)),
            _row_spec(d, lambda bb, i: (0, 0, 0, 0)),
        ] + weights,
        out_specs=[pl.BlockSpec((1, tm, NA_WIDTH), tile), pl.BlockSpec((1, tm, SSM_INNER), tile)]
        + [pl.BlockSpec((1, tm, wd), tile) for wd in widths],
        out_shape=[jax.ShapeDtypeStruct((b, s, NA_WIDTH), BF16), jax.ShapeDtypeStruct((b, s, SSM_INNER), BF16)]
        + [jax.ShapeDtypeStruct((b, s, wd), dt_) for wd, dt_ in zip(widths, dtypes)],
        scratch_shapes=[pltpu.VMEM((tm + 2 * HALO, d), BF16)],
        compiler_params=_cparams(2),
        name="inproj",
    )(x, x, x, mod4, mod4, norm4, w, cw, cb)

    kc, vc, xbc_c, dt_c = pl.pallas_call(
        _inproj_ctx_kernel,
        grid=(b, 1),
        in_specs=[
            pl.BlockSpec((1, l, d), tile),
            _row_spec(d, lambda bb, i: (0, b, 0, 0)),
            _row_spec(d, lambda bb, i: (0, b, 0, 1)),
            _row_spec(d, lambda bb, i: (0, 0, 0, 0)),
        ] + weights,
        out_specs=[pl.BlockSpec((1, l, wd), tile) for wd in widths],
        out_shape=[jax.ShapeDtypeStruct((b, l, wd), dt_) for wd, dt_ in zip(widths, dtypes)],
        scratch_shapes=[pltpu.VMEM((l + 2 * HALO, d), BF16)],
        compiler_params=_cparams(2),
        name="inproj_ctx",
    )(ctx, mod4, mod4, norm4, w, cw, cb)
    return (q, z, k, v, xbc, dt), (kc, vc, xbc_c, dt_c)


def _na_bias_tiles(rpb):
    h, n_rel_r, n_rel_c = rpb.shape
    w = GRID_W
    c = np.arange(w)[:, None]
    kc = np.arange(w)[None, :]
    cs = np.clip(c - NA_WIN_COLS // 2, 0, w - NA_WIN_COLS)
    col_ok = (kc >= cs) & (kc < cs + NA_WIN_COLS)
    pick = ((kc - c + NA_WIN_COLS - 1)[None] == np.arange(n_rel_c)[:, None, None]) & col_ok[None]
    t = jnp.einsum("hrd,dck->hrck", rpb, jnp.asarray(pick, F32), precision=lax.Precision.HIGHEST)
    t = jnp.where(jnp.asarray(col_ok)[None, None], t, NEG)
    n_after = NA_NTILES + 1 - NA_DOFF - n_rel_r
    tz = jnp.concatenate(
        [jnp.full((h, NA_DOFF, w, w), NEG, F32), t, jnp.full((h, n_after, w, w), NEG, F32)], axis=1)
    return jnp.concatenate([tz[:, :-1], tz[:, 1:]], axis=-1)


def _na_window_rows(kind, rows):
    r0 = {"first": 0, "interior": NA_QROWS, "last": rows - NA_QROWS}[kind]
    k0 = min(max(r0 - NA_WIN_ROWS // 2, 0), rows - NA_KROWS)
    los = [min(max(r0 + i - NA_WIN_ROWS // 2, 0), rows - NA_WIN_ROWS) - k0 for i in range(NA_QROWS)]
    return k0 - r0, los


def _na_block(q_ref, k_ref, v_ref, kc_ref, vc_ref, pt_ref, o_ref, sub, kind, rows):
    nk = NA_KROWS * GRID_W
    nq = NA_QROWS * GRID_W
    w = GRID_W
    delta, los = _na_window_rows(kind, rows)
    r0 = (pl.program_id(2) * NA_STEP_BLOCKS + sub) * NA_QROWS
    kstart = pl.multiple_of((r0 + delta) * GRID_W, GRID_W)
    q = q_ref[0, sub * nq:(sub + 1) * nq, :]
    kw = k_ref[0, pl.ds(kstart, nk), :]
    vw = v_ref[0, pl.ds(kstart, nk), :]
    kc = kc_ref[0]
    vc = vc_ref[0]
    lane = lax.broadcasted_iota(jnp.int32, (1, LANES), 1)
    one = jnp.ones((), v_ref.dtype)
    scale = jnp.asarray(NA_HEAD_DIM ** -0.5, q_ref.dtype)
    pairs = [(lo // 2, (lo + NA_WIN_ROWS - 1) // 2) for lo in los]
    gsz = NA_QROWS // NA_ROW_GROUPS
    out = jnp.zeros((nq, LANES), F32)
    for hh in range(LANES // NA_HEAD_DIM):
        sel = jnp.logical_and(lane >= hh * NA_HEAD_DIM, lane < (hh + 1) * NA_HEAD_DIM)
        qm = jnp.where(sel, q, jnp.zeros_like(q)) * scale
        vw_h = jnp.where(sel, vw, one)
        vc_h = jnp.where(sel, vc, one)
        o_groups = []
        for grp in range(NA_ROW_GROUPS):
            irange = range(grp * gsz, (grp + 1) * gsz)
            p_lo = min(pairs[i][0] for i in irange)
            p_hi = max(pairs[i][1] for i in irange)
            qg = qm[grp * gsz * w:(grp + 1) * gsz * w, :]
            s_raw = _dot_nt(qg, kw[p_lo * 2 * w:(p_hi + 1) * 2 * w, :])
            sc = _dot_nt(qg, kc)
            p_rows, pc_rows = [], []
            for i in irange:
                lo = los[i]
                jj_lo, jj_hi = pairs[i]
                ri = i - grp * gsz
                blocks = []
                for jj in range(jj_lo, jj_hi + 1):
                    t_idx = delta + 2 * jj - i + (NA_WIN_ROWS - 1) + NA_DOFF
                    blk = s_raw[ri * w:(ri + 1) * w, (jj - p_lo) * 2 * w:(jj - p_lo + 1) * 2 * w] + pt_ref[hh, t_idx]
                    left_out, right_out = 2 * jj < lo, 2 * jj + 1 >= lo + NA_WIN_ROWS
                    if left_out or right_out:
                        blk = blk + jnp.where(lane < w, NEG if left_out else 0.0, NEG if right_out else 0.0)
                    blocks.append(blk)
                s_i = jnp.concatenate(blocks, axis=1)
                sc_i = sc[ri * w:(ri + 1) * w, :]
                m = jnp.maximum(jnp.max(s_i, axis=-1, keepdims=True), jnp.max(sc_i, axis=-1, keepdims=True))
                pieces = [jnp.exp(s_i - m).astype(BF16)]
                if jj_lo > p_lo:
                    pieces.insert(0, jnp.zeros((w, (jj_lo - p_lo) * 2 * w), BF16))
                if jj_hi < p_hi:
                    pieces.append(jnp.zeros((w, (p_hi - jj_hi) * 2 * w), BF16))
                p_rows.append(jnp.concatenate(pieces, axis=1) if len(pieces) > 1 else pieces[0])
                pc_rows.append(jnp.exp(sc_i - m).astype(BF16))
            p = jnp.concatenate(p_rows, axis=0)
            pc = jnp.concatenate(pc_rows, axis=0)
            v_all = jnp.concatenate([vw_h[p_lo * 2 * w:(p_hi + 1) * 2 * w, :], vc_h], axis=0)
            o_groups.append(_dot(jnp.concatenate([p, pc], axis=1), v_all))
        o = jnp.concatenate(o_groups, axis=0)
        den = pltpu.roll(o, NA_HEAD_DIM, 1)
        out = jnp.where(sel, o / den, out)
    o_ref[0, sub * nq:(sub + 1) * nq, :] = out.astype(o_ref.dtype)


def _na_kernel(q_ref, k_ref, v_ref, kc_ref, vc_ref, pt_ref, o_ref, *, rows):
    step = pl.program_id(2)
    nsteps = rows // (NA_QROWS * NA_STEP_BLOCKS)

    def run(kinds):
        for sub, kind in enumerate(kinds):
            _na_block(q_ref, k_ref, v_ref, kc_ref, vc_ref, pt_ref, o_ref, sub, kind, rows)

    inner = ("interior",) * (NA_STEP_BLOCKS - 1)
    if nsteps == 1:
        run(("first",) + ("interior",) * (NA_STEP_BLOCKS - 2) + ("last",))
    else:
        pl.when(step == 0)(lambda: run(("first",) + inner))
        pl.when(jnp.logical_and(step > 0, step < nsteps - 1))(lambda: run(inner + ("interior",)))
        pl.when(step == nsteps - 1)(lambda: run(inner + ("last",)))


def _na_call(q, k, v, kc, vc, pt):
    b, s, _ = q.shape
    l = kc.shape[1]
    rows = s // GRID_W
    nq = NA_STEP_BLOCKS * NA_QROWS * GRID_W
    hp = LANES // NA_HEAD_DIM
    per_batch = lambda p, bb, rb: (bb, 0, p)
    return pl.pallas_call(
        functools.partial(_na_kernel, rows=rows),
        grid=(NA_WIDTH // LANES, b, s // nq),
        in_specs=[
            pl.BlockSpec((1, nq, LANES), lambda p, bb, rb: (bb, rb, p)),
            pl.BlockSpec((1, s, LANES), per_batch),
            pl.BlockSpec((1, s, LANES), per_batch),
            pl.BlockSpec((1, l, LANES), per_batch),
            pl.BlockSpec((1, l, LANES), per_batch),
            pl.BlockSpec((hp,) + pt.shape[1:], lambda p, bb, rb: (p, 0, 0, 0)),
        ],
        out_specs=pl.BlockSpec((1, nq, LANES), lambda p, bb, rb: (bb, rb, p)),
        out_shape=jax.ShapeDtypeStruct((b, s, NA_WIDTH), BF16),
        compiler_params=_cparams(3),
        name="nbr_attention",
    )(q, k, v, kc, vc, pt)


def _ssd_decays(dt_ref, prm_ref, row0, *, reverse):
    q = SSD_CHUNK
    dt = jax.nn.softplus(dt_ref[0, row0:row0 + q, :] + prm_ref[0:1, :])
    a = -dt * jnp.exp(prm_ref[1:2, :])
    ri = lax.broadcasted_iota(jnp.int32, (q, q), 0)
    ci = lax.broadcasted_iota(jnp.int32, (q, q), 1)
    tri = (ci >= ri) if reverse else (ci <= ri)
    ones = jnp.where(tri, 1.0, 0.0).astype(BF16)
    cs = jnp.zeros((q, LANES), F32)
    for part in _split3(a):
        cs = cs + _dot(ones, part)
    tot_row = cs[q - 1:q, :] if not reverse else cs[0:1, :]
    cs_t = cs.T
    dt_t = dt.T
    e_end_t = (jnp.exp(tot_row.T - cs_t) * dt_t).astype(BF16)
    seg_row = cs_t - jnp.log(dt_t)
    e_start = jnp.exp(cs)
    dec = jnp.exp(tot_row)
    return tri, cs, seg_row, e_end_t, e_start, dec


def _ssd_direction(xbc_ref, row0, decays, dsk_ref, state_ref, y_ref, *, reverse, emit_y):
    q = SSD_CHUNK
    tri, cs, seg_row, e_end_t, e_start, dec = decays
    lane_off = SSM_HEADS if reverse else 0
    lane = lax.broadcasted_iota(jnp.int32, (1, LANES), 1)
    lo = lane < SSM_HEAD_DIM
    rows = slice(row0, row0 + q)
    for g in range(SSM_GROUPS):
        bm = xbc_ref[0, rows, SSM_INNER + g * SSM_STATE:SSM_INNER + (g + 1) * SSM_STATE]
        cm = xbc_ref[0, rows, SSM_INNER + SSM_BC + g * SSM_STATE:SSM_INNER + SSM_BC + (g + 1) * SSM_STATE]
        bt = bm.astype(F32).T.astype(BF16)
        if emit_y:
            cb = _dot_nt(cm, bm).astype(BF16)
            cmf = cm.astype(F32)
        heads_per_group = SSM_HEADS // SSM_GROUPS
        for pp in range(heads_per_group // 2):
            pair = g * (heads_per_group // 2) + pp
            xs = xbc_ref[0, rows, pair * LANES:(pair + 1) * LANES]
            xs_lo = jnp.where(lo, xs, jnp.zeros_like(xs))
            xs_hi = jnp.where(lo, jnp.zeros_like(xs), xs)
            rhs_x = jnp.concatenate([xs_lo, xs_hi], axis=0)
            st = state_ref[pair]
            h0 = lane_off + 2 * pair
            btw = [bt * e_end_t[h0 + u:h0 + u + 1, :] for u in range(2)]
            upd = _dot(jnp.concatenate(btw, axis=1), rhs_x)
            if emit_y:
                stb = st.astype(BF16)
                st_lo = jnp.where(lo, stb, jnp.zeros_like(stb))
                st_hi = jnp.where(lo, jnp.zeros_like(stb), stb)
                lhs = []
                for u in range(2):
                    hcol = cs[:, h0 + u:h0 + u + 1]
                    seg = jnp.exp(jnp.where(tri, hcol - seg_row[h0 + u:h0 + u + 1, :], NEG))
                    lhs.append(cb * seg.astype(BF16))
                for u in range(2):
                    lhs.append((cmf * e_start[:, h0 + u:h0 + u + 1]).astype(BF16))
                y = _dot(jnp.concatenate(lhs, axis=1), jnp.concatenate([rhs_x, st_lo, st_hi], axis=0))
                y = y + dsk_ref[:, pair * LANES:(pair + 1) * LANES] * xs.astype(F32)
                y_ref[0, rows, pair * LANES:(pair + 1) * LANES] = y.astype(y_ref.dtype)
            dpair = jnp.where(lo, dec[:, h0:h0 + 1], dec[:, h0 + 1:h0 + 2])
            state_ref[pair] = st * dpair + upd


def _ssd_kernel(xf_ref, dtf_ref, xb_ref, dtb_ref, xc_ref, dtc_ref, prm_ref, dsk_ref, yf_ref, yb_ref,
                sf_ref, sb_ref, *, n_ctx):
    s = pl.program_id(1)
    step = SSD_STEP_CHUNKS * SSD_CHUNK

    @pl.when(s == 0)
    def _():
        sf_ref[...] = jnp.zeros_like(sf_ref)
        sb_ref[...] = jnp.zeros_like(sb_ref)

    def run(x_f, dt_f, x_b, dt_b, base_f, base_b, emit_y):
        order_f = [c * SSD_CHUNK for c in range(SSD_STEP_CHUNKS)]
        order_b = order_f[::-1]
        dec_f = [_ssd_decays(dt_f, prm_ref, base_f + r, reverse=False) for r in order_f]
        dec_b = [_ssd_decays(dt_b, prm_ref, base_b + r, reverse=True) for r in order_b]
        for c in range(SSD_STEP_CHUNKS):
            _ssd_direction(x_f, base_f + order_f[c], dec_f[c], dsk_ref.at[0:1], sf_ref, yf_ref,
                           reverse=False, emit_y=emit_y)
            _ssd_direction(x_b, base_b + order_b[c], dec_b[c], dsk_ref.at[1:2], sb_ref, yb_ref,
                           reverse=True, emit_y=emit_y)

    for j in range(n_ctx):
        pl.when(s == j)(functools.partial(run, xc_ref, dtc_ref, xc_ref, dtc_ref,
                                          j * step, (n_ctx - 1 - j) * step, False))

    @pl.when(s >= n_ctx)
    def _():
        run(xf_ref, dtf_ref, xb_ref, dtb_ref, 0, 0, True)


def _ssd_call(xbc, dt, xbc_c, dt_c, prm, dsk):
    b, n_lat, c = xbc.shape
    l = xbc_c.shape[1]
    q = SSD_STEP_CHUNKS * SSD_CHUNK
    nl = n_lat // q
    n_ctx = l // q
    fwd = lambda bb, s: (bb, jnp.maximum(s - n_ctx, 0), 0)
    bwd = lambda bb, s: (bb, jnp.minimum(nl - 1 - (s - n_ctx), nl - 1), 0)
    whole = lambda bb, s: (bb, 0, 0)
    fixed = lambda bb, s: (0, 0)
    return pl.pallas_call(
        functools.partial(_ssd_kernel, n_ctx=n_ctx),
        grid=(b, n_ctx + nl),
        in_specs=[
            pl.BlockSpec((1, q, c), fwd),
            pl.BlockSpec((1, q, LANES), fwd),
            pl.BlockSpec((1, q, c), bwd),
            pl.BlockSpec((1, q, LANES), bwd),
            pl.BlockSpec((1, l, c), whole),
            pl.BlockSpec((1, l, LANES), whole),
            pl.BlockSpec((2, LANES), fixed),
            pl.BlockSpec((2, SSM_INNER), fixed),
        ],
        out_specs=[
            pl.BlockSpec((1, q, SSM_INNER), fwd),
            pl.BlockSpec((1, q, SSM_INNER), bwd),
        ],
        out_shape=[jax.ShapeDtypeStruct((b, n_lat, SSM_INNER), BF16)] * 2,
        scratch_shapes=[pltpu.VMEM((SSM_HEADS // 2, SSM_STATE, LANES), F32)] * 2,
        compiler_params=_cparams(2),
        name="ssd_scan",
    )(xbc, dt, xbc, dt, xbc_c, dt_c, prm, dsk)


def _outproj_kernel(x_ref, att_ref, yf_ref, yb_ref, z_ref, gate_ref, ng_ref, gpost_ref, w_ref, o_ref):
    y = yf_ref[0].astype(F32) + yb_ref[0].astype(F32)
    yz = y * _silu(z_ref[0].astype(F32))
    gw = SSM_INNER // SSM_GROUPS
    parts = []
    for g in range(SSM_GROUPS):
        seg = yz[:, g * gw:(g + 1) * gw]
        parts.append(seg * _rms_scale(seg))
    ssm = (jnp.concatenate(parts, axis=-1) * ng_ref[...]).astype(BF16)
    o = _dot(jnp.concatenate([att_ref[0], ssm], axis=-1), w_ref[...])
    o_ref[0] = _gated_norm_add(x_ref[0], o, gpost_ref[...], gate_ref[...])


def _outproj_call(x, att, yf, yb, z, mod4, norm4, ng, w):
    b, s, d = x.shape
    tm = ROW_TILE
    tile = lambda bb, i: (bb, i, 0)
    return pl.pallas_call(
        _outproj_kernel,
        grid=(b, s // tm),
        in_specs=[
            pl.BlockSpec((1, tm, d), tile),
            pl.BlockSpec((1, tm, NA_WIDTH), tile),
            pl.BlockSpec((1, tm, SSM_INNER), tile),
            pl.BlockSpec((1, tm, SSM_INNER), tile),
            pl.BlockSpec((1, tm, SSM_INNER), tile),
            _row_spec(d, lambda bb, i: (0, bb, 0, 2)),
            pl.BlockSpec((1, SSM_INNER), lambda bb, i: (0, 0)),
            _row_spec(d, lambda bb, i: (0, 1, 0, 0)),
            pl.BlockSpec((None,) + w.shape[1:], lambda bb, i: (0, 0, 0), pipeline_mode=pl.Buffered(1)),
        ],
        out_specs=pl.BlockSpec((1, tm, d), tile),
        out_shape=jax.ShapeDtypeStruct((b, s, d), F32),
        compiler_params=_cparams(2),
        name="outproj",
    )(x, att, yf, yb, z, mod4, ng, norm4, w)


def _ffn_kernel(x_ref, prev_ref, next_ref, sh_ref, sc_ref, gate_ref, gpre_ref, gpost_ref,
                wup_ref, cw_ref, cb_ref, wdn_ref, o_ref, h_scr, *, tn):
    i = pl.program_id(1)
    nt = pl.num_programs(1)
    tm = x_ref.shape[1]
    hidden = wdn_ref.shape[0]
    x = x_ref[0]

    def nm(xt):
        return _norm_mod(xt, gpre_ref[...], sc_ref[...], sh_ref[...]).astype(BF16)

    zero = jnp.zeros((), BF16)
    h_scr[0:HALO, :] = jnp.where(i > 0, nm(prev_ref[0]), zero)
    h_scr[HALO:HALO + tm, :] = nm(x)
    h_scr[HALO + tm:, :] = jnp.where(i < nt - 1, nm(next_ref[0]), zero)
    gated = []
    for j in range(hidden // tn):
        u = _dot(h_scr[...], wup_ref[:, j * tn:(j + 1) * tn])
        v = _dot(h_scr[HALO:HALO + tm, :], wup_ref[:, hidden + j * tn:hidden + (j + 1) * tn])
        cv = _conv3_rows(u, cw_ref[:, j * tn:(j + 1) * tn], tm) + cb_ref[:, j * tn:(j + 1) * tn]
        gl = 0.5 * cv * (1.0 + lax.erf(cv * 0.7071067811865476)) * v
        gated.append(gl.astype(BF16))
    acc = _dot(jnp.concatenate(gated, axis=1), wdn_ref[...])
    o_ref[0] = _gated_norm_add(x, acc, gpost_ref[...], gate_ref[...])


def _ffn_call(x, layer, mod4, norm4, wup, cw, cb, wdn):
    b, s, d = x.shape
    hidden = wdn.shape[1]
    tm = FFN_ROW_TILE
    tile = lambda bb, i: (bb, i, 0)
    at_layer = lambda bb, i: (layer, 0, 0)
    return pl.pallas_call(
        functools.partial(_ffn_kernel, tn=FFN_COL_TILE),
        grid=(b, s // tm),
        in_specs=[pl.BlockSpec((1, tm, d), tile)] + _halo_specs(tm, d, s) + [
            _row_spec(d, lambda bb, i: (layer, bb, 0, 3)),
            _row_spec(d, lambda bb, i: (layer, bb, 0, 4)),
            _row_spec(d, lambda bb, i: (layer, bb, 0, 5)),
            _row_spec(d, lambda bb, i: (layer, 2, 0, 0)),
            _row_spec(d, lambda bb, i: (layer, 3, 0, 0)),
            pl.BlockSpec((None, d, 2 * hidden), at_layer, pipeline_mode=pl.Buffered(1)),
            pl.BlockSpec((None,) + cw.shape[1:], at_layer),
            pl.BlockSpec((None, 1, hidden), at_layer),
            pl.BlockSpec((None, hidden, d), at_layer, pipeline_mode=pl.Buffered(1)),
        ],
        out_specs=pl.BlockSpec((1, tm, d), tile),
        out_shape=jax.ShapeDtypeStruct((b, s, d), F32),
        scratch_shapes=[pltpu.VMEM((tm + 2 * HALO, d), BF16)],
        compiler_params=_cparams(2),
        name="conv_ffn",
    )(x, x, x, mod4, mod4, mod4, norm4, norm4, wup, cw, cb, wdn)


def _window_sum(h_ext, w, rows):
    n = h_ext.shape[0]
    acc = h_ext
    span = 1
    while span < w:
        acc = acc + pltpu.roll(acc, span, 0)
        span *= 2
    shift = w // 2 - 1
    if shift:
        acc = pltpu.roll(acc, n - shift, 0)
    return acc[HALO:HALO + rows]


def _pool_kernel(x_ref, prev_ref, next_ref, sh_ref, sc_ref, gate_ref, gpre_ref, gpost_ref,
                 pw_ref, pb_ref, ps_ref, o_ref, h_scr, *, seq):
    i = pl.program_id(1)
    nt = pl.num_programs(1)
    ts = x_ref.shape[1]
    x = x_ref[0]
    gwidth = pw_ref.shape[1]

    def nm(xt):
        return _norm_mod(xt, gpre_ref[...], sc_ref[...], sh_ref[...])

    h_scr[0:HALO, :] = jnp.where(i > 0, nm(prev_ref[0]), 0.0)
    h_scr[HALO:HALO + ts, :] = nm(x)
    h_scr[HALO + ts:, :] = jnp.where(i < nt - 1, nm(next_ref[0]), 0.0)
    t = i * ts + lax.broadcasted_iota(jnp.int32, (ts, 1), 0)
    ys = []
    for gi, w in enumerate(POOL_WINDOWS):
        cols = slice(gi * gwidth, (gi + 1) * gwidth)
        h_ext = h_scr[:, cols]
        cnt = (jnp.minimum(t + w // 2, seq) - jnp.maximum(t - w // 2, 0)).astype(F32)
        pooled = _window_sum(h_ext, w, ts) / cnt - h_ext[HALO:HALO + ts]
        ys.append(_dot(pooled.astype(BF16), pw_ref[gi]) + pb_ref[:, cols])
    y = jnp.concatenate(ys, axis=-1) * ps_ref[...]
    o_ref[0] = _gated_norm_add(x, y, gpost_ref[...], gate_ref[...])


def _pool_call(x, layer, mod4, norm4, pw, pb, ps):
    b, s, d = x.shape
    ts = ROW_TILE
    tile = lambda bb, i: (bb, i, 0)
    fixed = lambda bb, i: (0, 0)
    return pl.pallas_call(
        functools.partial(_pool_kernel, seq=s),
        grid=(b, s // ts),
        in_specs=[pl.BlockSpec((1, ts, d), tile)] + _halo_specs(ts, d, s) + [
            _row_spec(d, lambda bb, i: (layer, bb, 0, 0)),
            _row_spec(d, lambda bb, i: (layer, bb, 0, 1)),
            _row_spec(d, lambda bb, i: (layer, bb, 0, 2)),
            _row_spec(d, lambda bb, i: (layer, 0, 0, 0)),
            _row_spec(d, lambda bb, i: (layer, 1, 0, 0)),
            pl.BlockSpec(pw.shape, lambda bb, i: (0, 0, 0)),
            pl.BlockSpec((1, d), fixed),
            pl.BlockSpec((1, d), fixed),
        ],
        out_specs=pl.BlockSpec((1, ts, d), tile),
        out_shape=jax.ShapeDtypeStruct((b, s, d), F32),
        scratch_shapes=[pltpu.VMEM((ts + 2 * HALO, d), F32)],
        compiler_params=_cparams(2),
        name="pool_mixer",
    )(x, x, x, mod4, mod4, mod4, norm4, norm4, pw, pb, ps)


def kernel(x, c, ctx, c_ctx, ada_w, ada_b, norm_g, w_in, w_out, na_rpb, ssm_conv_w, ssm_conv_b, ssm_a_log, ssm_dt_bias, ssm_d, ssm_norm_g, pool_w, pool_b, pool_scale, ffn_w_up, ffn_conv_w, ffn_conv_b, ffn_w_down):
    b, s, d = x.shape
    l = ctx.shape[1]
    depth = ada_w.shape[0]
    hidden = ffn_w_down.shape[1]
    rows = s // GRID_W
    assert depth == 2 and l % (SSD_STEP_CHUNKS * SSD_CHUNK) == 0 and s % l == 0 and s % (NA_STEP_BLOCKS * NA_QROWS * GRID_W) == 0
    assert s % FFN_ROW_TILE == 0 and s % ROW_TILE == 0 and hidden % FFN_COL_TILE == 0
    assert rows >= NA_KROWS and 2 * SSM_HEADS <= LANES
    assert NA_DOFF + 2 * NA_WIN_ROWS - 1 <= NA_NTILES + 1 and NA_KROWS + NA_QROWS + NA_DOFF <= NA_NTILES + 1

    rows_c = -(-(b + 1) // 8) * 8
    c_ext = jnp.zeros((rows_c, d), F32).at[:b].set(c).at[b].set(c_ctx)
    mod4 = _ada_call(c_ext, ada_w, ada_b).reshape(depth, rows_c, 1, 6 * d)
    norm4 = norm_g.reshape(depth, 4, 1, d)
    wup = ffn_w_up.astype(BF16)
    wdn = ffn_w_down.astype(BF16)
    ffn_cb = ffn_conv_b.reshape(depth, 1, hidden)

    w_cat = jnp.pad(w_in, ((0, 0), (0, 0), (0, LANES - 2 * SSM_HEADS))).astype(BF16)
    (q, z, k, v, xbc, dt_raw), (kc, vc, xbc_c, dt_c) = _inproj_call(
        x, ctx, mod4, norm4, w_cat, ssm_conv_w, ssm_conv_b.reshape(1, 1, SSM_CONV_DIM))
    att = _na_call(q, k, v, kc, vc, _na_bias_tiles(na_rpb[0]))
    pad = LANES - 2 * SSM_HEADS
    prm = jnp.stack([jnp.pad(ssm_dt_bias[0].reshape(-1), (0, pad)), jnp.pad(ssm_a_log[0].reshape(-1), (0, pad))])
    dsk = jnp.repeat(ssm_d[0], SSM_HEAD_DIM, axis=1)
    y_f, y_b = _ssd_call(xbc, dt_raw, xbc_c, dt_c, prm, dsk)
    x = _outproj_call(x, att, y_f, y_b, z, mod4, norm4, ssm_norm_g.reshape(1, SSM_INNER), w_out.astype(BF16))
    x = _ffn_call(x, 0, mod4, norm4, wup, ffn_conv_w, ffn_cb, wdn)

    x = _pool_call(x, 1, mod4, norm4, pool_w[0].astype(BF16), pool_b.reshape(1, d), pool_scale.reshape(1, d))
    x = _ffn_call(x, 1, mod4, norm4, wup, ffn_conv_w, ffn_cb, wdn)
    return x
```

```python
import functools

import numpy as np
import jax
import jax.numpy as jnp
from jax import lax
from jax.experimental import pallas as pl
from jax.experimental.pallas import tpu as pltpu

F32 = jnp.float32
BF16 = jnp.bfloat16

GRID_W = 64
NA_HEADS = 8
NA_HEAD_DIM = 64
NA_WIDTH = NA_HEADS * NA_HEAD_DIM
NA_WIN_ROWS = 8
NA_WIN_COLS = 16
SSM_HEADS = 16
SSM_HEAD_DIM = 64
SSM_INNER = SSM_HEADS * SSM_HEAD_DIM
SSM_GROUPS = 4
SSM_STATE = 128
SSD_CHUNK = 128
SSD_STEP_CHUNKS = 2
SSM_BC = SSM_GROUPS * SSM_STATE
SSM_CONV_DIM = SSM_INNER + 2 * SSM_BC
POOL_WINDOWS = (2, 4, 8, 16)
RMS_EPS = 1e-6

LANES = 128
HALO = 16
NA_QROWS = 8
NA_KROWS = 16
NA_STEP_BLOCKS = 4
NA_ROW_GROUPS = 2
NA_DOFF = NA_QROWS
NA_NTILES = 32
NEG = -1e30
VMEM_LIMIT = 56 * 1024 * 1024
ROW_TILE = 1024
FFN_ROW_TILE = 1024
FFN_COL_TILE = 256


def _cparams(n_axes):
    return pltpu.CompilerParams(dimension_semantics=("arbitrary",) * n_axes, vmem_limit_bytes=VMEM_LIMIT)


def _rms_scale(x):
    return lax.rsqrt(jnp.mean(x * x, axis=-1, keepdims=True) + RMS_EPS)


def _norm_mod(x, g, sc, sh):
    return x * _rms_scale(x) * (g * (1.0 + sc)) + sh


def _gated_norm_add(x, y, g, gate):
    return x + y * _rms_scale(y) * (g * gate)


def _silu(x):
    return x * jax.nn.sigmoid(x)


def _dot(a, b):
    return jnp.dot(a, b, preferred_element_type=F32)


def _dot_nt(a, b):
    return lax.dot_general(a, b, (((1,), (1,)), ((), ())), preferred_element_type=F32)


def _split3(a):
    hi = a.astype(BF16)
    r = a - hi.astype(F32)
    mid = r.astype(BF16)
    lo = (r - mid.astype(F32)).astype(BF16)
    return hi, mid, lo


def _conv3_rows(u, w_rows, rows):
    n = u.shape[0]
    u_prev = pltpu.roll(u, 1, 0)[HALO:HALO + rows]
    u_next = pltpu.roll(u, n - 1, 0)[HALO:HALO + rows]
    return u_prev * w_rows[0:1] + u[HALO:HALO + rows] * w_rows[1:2] + u_next * w_rows[2:3]


def _row_spec(d, index):
    return pl.BlockSpec((None, None, 1, d), index)


def _halo_specs(tm, d, s):
    rh = tm // HALO
    return [
        pl.BlockSpec((1, HALO, d), lambda bb, i: (bb, jnp.maximum(i * rh - 1, 0), 0)),
        pl.BlockSpec((1, HALO, d), lambda bb, i: (bb, jnp.minimum((i + 1) * rh, s // HALO - 1), 0)),
    ]


def _ada_kernel(c_ref, w_ref, b_ref, o_ref):
    s = _silu(c_ref[...])
    w = w_ref[0]
    acc = jnp.zeros(o_ref.shape[1:], F32)
    s_parts = _split3(s)[:2]
    w_parts = _split3(w)[:2]
    for si, sp in enumerate(s_parts):
        for wi, wp in enumerate(w_parts):
            if si + wi <= 1:
                acc = acc + _dot(sp, wp)
    o_ref[0] = acc + b_ref[0]


def _ada_call(c_ext, ada_w, ada_b):
    depth, d, n = ada_w.shape
    r = c_ext.shape[0]
    tn = 1536
    return pl.pallas_call(
        _ada_kernel,
        grid=(depth, n // tn),
        in_specs=[
            pl.BlockSpec((r, d), lambda l, j: (0, 0)),
            pl.BlockSpec((1, d, tn), lambda l, j: (l, 0, j)),
            pl.BlockSpec((1, 1, tn), lambda l, j: (l, 0, j)),
        ],
        out_specs=pl.BlockSpec((1, r, tn), lambda l, j: (l, 0, j)),
        out_shape=jax.ShapeDtypeStruct((depth, r, n), F32),
        compiler_params=_cparams(2),
        name="ada_mod",
    )(c_ext, ada_w, ada_b.reshape(depth, 1, n))


def _inproj_body(h_scr, w_ref, cw_ref, cb_ref, qz_refs, k_ref, v_ref, xbc_ref, dt_ref, tm):
    hb = h_scr[HALO:HALO + tm, :]
    if qz_refs is not None:
        q_ref, z_ref = qz_refs
        q_ref[0] = _dot(hb, w_ref[:, 0:NA_WIDTH]).astype(q_ref.dtype)
        z_ref[0] = _dot(hb, w_ref[:, NA_WIDTH:NA_WIDTH + SSM_INNER]).astype(z_ref.dtype)
    o = NA_WIDTH + SSM_INNER
    k_ref[0] = _dot(hb, w_ref[:, o:o + NA_WIDTH]).astype(k_ref.dtype)
    o += NA_WIDTH
    v_ref[0] = _dot(hb, w_ref[:, o:o + NA_WIDTH]).astype(v_ref.dtype)
    o += NA_WIDTH
    cblk = 512
    for c0 in range(0, SSM_CONV_DIM, cblk):
        u = _dot(h_scr[...], w_ref[:, o + c0:o + c0 + cblk])
        y = _conv3_rows(u, cw_ref[:, c0:c0 + cblk], tm) + cb_ref[:, c0:c0 + cblk]
        xbc_ref[0, :, c0:c0 + cblk] = _silu(y).astype(xbc_ref.dtype)
    o += SSM_CONV_DIM
    dt_ref[0] = _dot(hb, w_ref[:, o:o + LANES]).astype(dt_ref.dtype)


def _inproj_lat_kernel(x_ref, prev_ref, next_ref, sh_ref, sc_ref, g_ref, w_ref, cw_ref, cb_ref,
                       q_ref, z_ref, k_ref, v_ref, xbc_ref, dt_ref, h_scr):
    i = pl.program_id(1)
    nt = pl.num_programs(1)
    tm = x_ref.shape[1]

    def nm(xt):
        return _norm_mod(xt, g_ref[...], sc_ref[...], sh_ref[...]).astype(BF16)

    zero = jnp.zeros((), BF16)
    h_scr[0:HALO, :] = jnp.where(i > 0, nm(prev_ref[0]), zero)
    h_scr[HALO:HALO + tm, :] = nm(x_ref[0])
    h_scr[HALO + tm:, :] = jnp.where(i < nt - 1, nm(next_ref[0]), zero)
    _inproj_body(h_scr, w_ref, cw_ref, cb_ref, (q_ref, z_ref), k_ref, v_ref, xbc_ref, dt_ref, tm)


def _inproj_ctx_kernel(x_ref, sh_ref, sc_ref, g_ref, w_ref, cw_ref, cb_ref, k_ref, v_ref, xbc_ref, dt_ref, h_scr):
    tm = x_ref.shape[1]
    h_scr[0:HALO, :] = jnp.zeros((HALO, h_scr.shape[1]), BF16)
    h_scr[HALO:HALO + tm, :] = _norm_mod(x_ref[0], g_ref[...], sc_ref[...], sh_ref[...]).astype(BF16)
    h_scr[HALO + tm:, :] = jnp.zeros((HALO, h_scr.shape[1]), BF16)
    _inproj_body(h_scr, w_ref, cw_ref, cb_ref, None, k_ref, v_ref, xbc_ref, dt_ref, tm)


def _inproj_call(x, ctx, mod4, norm4, w, cw, cb):
    b, s, d = x.shape
    l = ctx.shape[1]
    tm = ROW_TILE
    n = w.shape[-1]
    tile = lambda bb, i: (bb, i, 0)
    fixed = lambda bb, i: (0, 0, 0)
    weights = [
        pl.BlockSpec((None, d, n), fixed, pipeline_mode=pl.Buffered(1)),
        pl.BlockSpec((None,) + cw.shape[1:], fixed),
        pl.BlockSpec((None,) + cb.shape[1:], fixed),
    ]
    widths = (NA_WIDTH, NA_WIDTH, SSM_CONV_DIM, LANES)
    dtypes = (BF16, BF16, BF16, F32)
    q, z, k, v, xbc, dt = pl.pallas_call(
        _inproj_lat_kernel,
        grid=(b, s // tm),
        in_specs=[pl.BlockSpec((1, tm, d), tile)] + _halo_specs(tm, d, s) + [
            _row_spec(d, lambda bb, i: (0, bb, 0, 0)),
            _row_spec(d, lambda bb, i: (0, bb, 0, 1)),
            _row_spec(d, lambda bb, i: (0, 0, 0, 0)),
        ] + weights,
        out_specs=[pl.BlockSpec((1, tm, NA_WIDTH), tile), pl.BlockSpec((1, tm, SSM_INNER), tile)]
        + [pl.BlockSpec((1, tm, wd), tile) for wd in widths],
        out_shape=[jax.ShapeDtypeStruct((b, s, NA_WIDTH), BF16), jax.ShapeDtypeStruct((b, s, SSM_INNER), BF16)]
        + [jax.ShapeDtypeStruct((b, s, wd), dt_) for wd, dt_ in zip(widths, dtypes)],
        scratch_shapes=[pltpu.VMEM((tm + 2 * HALO, d), BF16)],
        compiler_params=_cparams(2),
        name="inproj",
    )(x, x, x, mod4, mod4, norm4, w, cw, cb)

    kc, vc, xbc_c, dt_c = pl.pallas_call(
        _inproj_ctx_kernel,
        grid=(b, 1),
        in_specs=[
            pl.BlockSpec((1, l, d), tile),
            _row_spec(d, lambda bb, i: (0, b, 0, 0)),
            _row_spec(d, lambda bb, i: (0, b, 0, 1)),
            _row_spec(d, lambda bb, i: (0, 0, 0, 0)),
        ] + weights,
        out_specs=[pl.BlockSpec((1, l, wd), tile) for wd in widths],
        out_shape=[jax.ShapeDtypeStruct((b, l, wd), dt_) for wd, dt_ in zip(widths, dtypes)],
        scratch_shapes=[pltpu.VMEM((l + 2 * HALO, d), BF16)],
        compiler_params=_cparams(2),
        name="inproj_ctx",
    )(ctx, mod4, mod4, norm4, w, cw, cb)
    return (q, z, k, v, xbc, dt), (kc, vc, xbc_c, dt_c)


def _na_bias_tiles(rpb):
    h, n_rel_r, n_rel_c = rpb.shape
    w = GRID_W
    c = np.arange(w)[:, None]
    kc = np.arange(w)[None, :]
    cs = np.clip(c - NA_WIN_COLS // 2, 0, w - NA_WIN_COLS)
    col_ok = (kc >= cs) & (kc < cs + NA_WIN_COLS)
    pick = ((kc - c + NA_WIN_COLS - 1)[None] == np.arange(n_rel_c)[:, None, None]) & col_ok[None]
    t = jnp.einsum("hrd,dck->hrck", rpb, jnp.asarray(pick, F32), precision=lax.Precision.HIGHEST)
    t = jnp.where(jnp.asarray(col_ok)[None, None], t, NEG)
    n_after = NA_NTILES + 1 - NA_DOFF - n_rel_r
    tz = jnp.concatenate(
        [jnp.full((h, NA_DOFF, w, w), NEG, F32), t, jnp.full((h, n_after, w, w), NEG, F32)], axis=1)
    return jnp.concatenate([tz[:, :-1], tz[:, 1:]], axis=-1)


def _na_window_rows(kind, rows):
    r0 = {"first": 0, "interior": NA_QROWS, "last": rows - NA_QROWS}[kind]
    k0 = min(max(r0 - NA_WIN_ROWS // 2, 0), rows - NA_KROWS)
    los = [min(max(r0 + i - NA_WIN_ROWS // 2, 0), rows - NA_WIN_ROWS) - k0 for i in range(NA_QROWS)]
    return k0 - r0, los


def _na_block(q_ref, k_ref, v_ref, kc_ref, vc_ref, pt_ref, o_ref, sub, kind, rows):
    nk = NA_KROWS * GRID_W
    nq = NA_QROWS * GRID_W
    w = GRID_W
    delta, los = _na_window_rows(kind, rows)
    r0 = (pl.program_id(2) * NA_STEP_BLOCKS + sub) * NA_QROWS
    kstart = pl.multiple_of((r0 + delta) * GRID_W, GRID_W)
    q = q_ref[0, sub * nq:(sub + 1) * nq, :]
    kw = k_ref[0, pl.ds(kstart, nk), :]
    vw = v_ref[0, pl.ds(kstart, nk), :]
    kc = kc_ref[0]
    vc = vc_ref[0]
    lane = lax.broadcasted_iota(jnp.int32, (1, LANES), 1)
    one = jnp.ones((), v_ref.dtype)
    scale = jnp.asarray(NA_HEAD_DIM ** -0.5, q_ref.dtype)
    pairs = [(lo // 2, (lo + NA_WIN_ROWS - 1) // 2) for lo in los]
    gsz = NA_QROWS // NA_ROW_GROUPS
    out = jnp.zeros((nq, LANES), F32)
    for hh in range(LANES // NA_HEAD_DIM):
        sel = jnp.logical_and(lane >= hh * NA_HEAD_DIM, lane < (hh + 1) * NA_HEAD_DIM)
        qm = jnp.where(sel, q, jnp.zeros_like(q)) * scale
        vw_h = jnp.where(sel, vw, one)
        vc_h = jnp.where(sel, vc, one)
        o_groups = []
        for grp in range(NA_ROW_GROUPS):
            irange = range(grp * gsz, (grp + 1) * gsz)
            p_lo = min(pairs[i][0] for i in irange)
            p_hi = max(pairs[i][1] for i in irange)
            qg = qm[grp * gsz * w:(grp + 1) * gsz * w, :]
            s_raw = _dot_nt(qg, kw[p_lo * 2 * w:(p_hi + 1) * 2 * w, :])
            sc = _dot_nt(qg, kc)
            p_rows, pc_rows = [], []
            for i in irange:
                lo = los[i]
                jj_lo, jj_hi = pairs[i]
                ri = i - grp * gsz
                blocks = []
                for jj in range(jj_lo, jj_hi + 1):
                    t_idx = delta + 2 * jj - i + (NA_WIN_ROWS - 1) + NA_DOFF
                    blk = s_raw[ri * w:(ri + 1) * w, (jj - p_lo) * 2 * w:(jj - p_lo + 1) * 2 * w] + pt_ref[hh, t_idx]
                    left_out, right_out = 2 * jj < lo, 2 * jj + 1 >= lo + NA_WIN_ROWS
                    if left_out or right_out:
                        blk = blk + jnp.where(lane < w, NEG if left_out else 0.0, NEG if right_out else 0.0)
                    blocks.append(blk)
                s_i = jnp.concatenate(blocks, axis=1)
                sc_i = sc[ri * w:(ri + 1) * w, :]
                m = jnp.maximum(jnp.max(s_i, axis=-1, keepdims=True), jnp.max(sc_i, axis=-1, keepdims=True))
                pieces = [jnp.exp(s_i - m).astype(BF16)]
                if jj_lo > p_lo:
                    pieces.insert(0, jnp.zeros((w, (jj_lo - p_lo) * 2 * w), BF16))
                if jj_hi < p_hi:
                    pieces.append(jnp.zeros((w, (p_hi - jj_hi) * 2 * w), BF16))
                p_rows.append(jnp.concatenate(pieces, axis=1) if len(pieces) > 1 else pieces[0])
                pc_rows.append(jnp.exp(sc_i - m).astype(BF16))
            p = jnp.concatenate(p_rows, axis=0)
            pc = jnp.concatenate(pc_rows, axis=0)
            v_all = jnp.concatenate([vw_h[p_lo * 2 * w:(p_hi + 1) * 2 * w, :], vc_h], axis=0)
            o_groups.append(_dot(jnp.concatenate([p, pc], axis=1), v_all))
        o = jnp.concatenate(o_groups, axis=0)
        den = pltpu.roll(o, NA_HEAD_DIM, 1)
        out = jnp.where(sel, o / den, out)
    o_ref[0, sub * nq:(sub + 1) * nq, :] = out.astype(o_ref.dtype)


def _na_kernel(q_ref, k_ref, v_ref, kc_ref, vc_ref, pt_ref, o_ref, *, rows):
    step = pl.program_id(2)
    nsteps = rows // (NA_QROWS * NA_STEP_BLOCKS)

    def run(kinds):
        for sub, kind in enumerate(kinds):
            _na_block(q_ref, k_ref, v_ref, kc_ref, vc_ref, pt_ref, o_ref, sub, kind, rows)

    inner = ("interior",) * (NA_STEP_BLOCKS - 1)
    if nsteps == 1:
        run(("first",) + ("interior",) * (NA_STEP_BLOCKS - 2) + ("last",))
    else:
        pl.when(step == 0)(lambda: run(("first",) + inner))
        pl.when(jnp.logical_and(step > 0, step < nsteps - 1))(lambda: run(inner + ("interior",)))
        pl.when(step == nsteps - 1)(lambda: run(inner + ("last",)))


def _na_call(q, k, v, kc, vc, pt):
    b, s, _ = q.shape
    l = kc.shape[1]
    rows = s // GRID_W
    nq = NA_STEP_BLOCKS * NA_QROWS * GRID_W
    hp = LANES // NA_HEAD_DIM
    per_batch = lambda p, bb, rb: (bb, 0, p)
    return pl.pallas_call(
        functools.partial(_na_kernel, rows=rows),
        grid=(NA_WIDTH // LANES, b, s // nq),
        in_specs=[
            pl.BlockSpec((1, nq, LANES), lambda p, bb, rb: (bb, rb, p)),
            pl.BlockSpec((1, s, LANES), per_batch),
            pl.BlockSpec((1, s, LANES), per_batch),
            pl.BlockSpec((1, l, LANES), per_batch),
            pl.BlockSpec((1, l, LANES), per_batch),
            pl.BlockSpec((hp,) + pt.shape[1:], lambda p, bb, rb: (p, 0, 0, 0)),
        ],
        out_specs=pl.BlockSpec((1, nq, LANES), lambda p, bb, rb: (bb, rb, p)),
        out_shape=jax.ShapeDtypeStruct((b, s, NA_WIDTH), BF16),
        compiler_params=_cparams(3),
        name="nbr_attention",
    )(q, k, v, kc, vc, pt)


def _ssd_decays(dt_ref, prm_ref, row0, *, reverse):
    q = SSD_CHUNK
    dt = jax.nn.softplus(dt_ref[0, row0:row0 + q, :] + prm_ref[0:1, :])
    a = -dt * jnp.exp(prm_ref[1:2, :])
    ri = lax.broadcasted_iota(jnp.int32, (q, q), 0)
    ci = lax.broadcasted_iota(jnp.int32, (q, q), 1)
    tri = (ci >= ri) if reverse else (ci <= ri)
    ones = jnp.where(tri, 1.0, 0.0).astype(BF16)
    cs = jnp.zeros((q, LANES), F32)
    for part in _split3(a):
        cs = cs + _dot(ones, part)
    tot_row = cs[q - 1:q, :] if not reverse else cs[0:1, :]
    cs_t = cs.T
    dt_t = dt.T
    e_end_t = (jnp.exp(tot_row.T - cs_t) * dt_t).astype(BF16)
    seg_row = cs_t - jnp.log(dt_t)
    e_start = jnp.exp(cs)
    dec = jnp.exp(tot_row)
    return tri, cs, seg_row, e_end_t, e_start, dec


def _ssd_direction(xbc_ref, row0, decays, dsk_ref, state_ref, y_ref, *, reverse, emit_y):
    q = SSD_CHUNK
    tri, cs, seg_row, e_end_t, e_start, dec = decays
    lane_off = SSM_HEADS if reverse else 0
    lane = lax.broadcasted_iota(jnp.int32, (1, LANES), 1)
    lo = lane < SSM_HEAD_DIM
    rows = slice(row0, row0 + q)
    for g in range(SSM_GROUPS):
        bm = xbc_ref[0, rows, SSM_INNER + g * SSM_STATE:SSM_INNER + (g + 1) * SSM_STATE]
        cm = xbc_ref[0, rows, SSM_INNER + SSM_BC + g * SSM_STATE:SSM_INNER + SSM_BC + (g + 1) * SSM_STATE]
        bt = bm.astype(F32).T.astype(BF16)
        if emit_y:
            cb = _dot_nt(cm, bm).astype(BF16)
            cmf = cm.astype(F32)
        heads_per_group = SSM_HEADS // SSM_GROUPS
        for pp in range(heads_per_group // 2):
            pair = g * (heads_per_group // 2) + pp
            xs = xbc_ref[0, rows, pair * LANES:(pair + 1) * LANES]
            xs_lo = jnp.where(lo, xs, jnp.zeros_like(xs))
            xs_hi = jnp.where(lo, jnp.zeros_like(xs), xs)
            rhs_x = jnp.concatenate([xs_lo, xs_hi], axis=0)
            st = state_ref[pair]
            h0 = lane_off + 2 * pair
            btw = [bt * e_end_t[h0 + u:h0 + u + 1, :] for u in range(2)]
            upd = _dot(jnp.concatenate(btw, axis=1), rhs_x)
            if emit_y:
                stb = st.astype(BF16)
                st_lo = jnp.where(lo, stb, jnp.zeros_like(stb))
                st_hi = jnp.where(lo, jnp.zeros_like(stb), stb)
                lhs = []
                for u in range(2):
                    hcol = cs[:, h0 + u:h0 + u + 1]
                    seg = jnp.exp(jnp.where(tri, hcol - seg_row[h0 + u:h0 + u + 1, :], NEG))
                    lhs.append(cb * seg.astype(BF16))
                for u in range(2):
                    lhs.append(cm * e_start[:, h0 + u:h0 + u + 1].astype(BF16))
                y = _dot(jnp.concatenate(lhs, axis=1), jnp.concatenate([rhs_x, st_lo, st_hi], axis=0))
                y = y + dsk_ref[:, pair * LANES:(pair + 1) * LANES] * xs.astype(F32)
                y_ref[0, rows, pair * LANES:(pair + 1) * LANES] = y.astype(y_ref.dtype)
            dpair = jnp.where(lo, dec[:, h0:h0 + 1], dec[:, h0 + 1:h0 + 2])
            state_ref[pair] = st * dpair + upd


def _ssd_kernel(xf_ref, dtf_ref, xb_ref, dtb_ref, xc_ref, dtc_ref, prm_ref, dsk_ref, yf_ref, yb_ref,
                sf_ref, sb_ref, *, n_ctx):
    s = pl.program_id(1)
    step = SSD_STEP_CHUNKS * SSD_CHUNK

    @pl.when(s == 0)
    def _():
        sf_ref[...] = jnp.zeros_like(sf_ref)
        sb_ref[...] = jnp.zeros_like(sb_ref)

    def run(x_f, dt_f, x_b, dt_b, base_f, base_b, emit_y):
        order_f = [c * SSD_CHUNK for c in range(SSD_STEP_CHUNKS)]
        order_b = order_f[::-1]
        dec_f = [_ssd_decays(dt_f, prm_ref, base_f + r, reverse=False) for r in order_f]
        dec_b = [_ssd_decays(dt_b, prm_ref, base_b + r, reverse=True) for r in order_b]
        for c in range(SSD_STEP_CHUNKS):
            _ssd_direction(x_f, base_f + order_f[c], dec_f[c], dsk_ref.at[0:1], sf_ref, yf_ref,
                           reverse=False, emit_y=emit_y)
            _ssd_direction(x_b, base_b + order_b[c], dec_b[c], dsk_ref.at[1:2], sb_ref, yb_ref,
                           reverse=True, emit_y=emit_y)

    for j in range(n_ctx):
        pl.when(s == j)(functools.partial(run, xc_ref, dtc_ref, xc_ref, dtc_ref,
                                          j * step, (n_ctx - 1 - j) * step, False))

    @pl.when(s >= n_ctx)
    def _():
        run(xf_ref, dtf_ref, xb_ref, dtb_ref, 0, 0, True)


def _ssd_call(xbc, dt, xbc_c, dt_c, prm, dsk):
    b, n_lat, c = xbc.shape
    l = xbc_c.shape[1]
    q = SSD_STEP_CHUNKS * SSD_CHUNK
    nl = n_lat // q
    n_ctx = l // q
    fwd = lambda bb, s: (bb, jnp.maximum(s - n_ctx, 0), 0)
    bwd = lambda bb, s: (bb, jnp.minimum(nl - 1 - (s - n_ctx), nl - 1), 0)
    whole = lambda bb, s: (bb, 0, 0)
    fixed = lambda bb, s: (0, 0)
    return pl.pallas_call(
        functools.partial(_ssd_kernel, n_ctx=n_ctx),
        grid=(b, n_ctx + nl),
        in_specs=[
            pl.BlockSpec((1, q, c), fwd),
            pl.BlockSpec((1, q, LANES), fwd),
            pl.BlockSpec((1, q, c), bwd),
            pl.BlockSpec((1, q, LANES), bwd),
            pl.BlockSpec((1, l, c), whole),
            pl.BlockSpec((1, l, LANES), whole),
            pl.BlockSpec((2, LANES), fixed),
            pl.BlockSpec((2, SSM_INNER), fixed),
        ],
        out_specs=[
            pl.BlockSpec((1, q, SSM_INNER), fwd),
            pl.BlockSpec((1, q, SSM_INNER), bwd),
        ],
        out_shape=[jax.ShapeDtypeStruct((b, n_lat, SSM_INNER), BF16)] * 2,
        scratch_shapes=[pltpu.VMEM((SSM_HEADS // 2, SSM_STATE, LANES), F32)] * 2,
        compiler_params=_cparams(2),
        name="ssd_scan",
    )(xbc, dt, xbc, dt, xbc_c, dt_c, prm, dsk)


def _outproj_kernel(x_ref, att_ref, yf_ref, yb_ref, z_ref, gate_ref, ng_ref, gpost_ref, w_ref, o_ref):
    y = yf_ref[0].astype(F32) + yb_ref[0].astype(F32)
    yz = y * _silu(z_ref[0].astype(F32))
    gw = SSM_INNER // SSM_GROUPS
    parts = []
    for g in range(SSM_GROUPS):
        seg = yz[:, g * gw:(g + 1) * gw]
        parts.append(seg * _rms_scale(seg))
    ssm = (jnp.concatenate(parts, axis=-1) * ng_ref[...]).astype(BF16)
    o = _dot(jnp.concatenate([att_ref[0], ssm], axis=-1), w_ref[...])
    o_ref[0] = _gated_norm_add(x_ref[0], o, gpost_ref[...], gate_ref[...])


def _outproj_call(x, att, yf, yb, z, mod4, norm4, ng, w):
    b, s, d = x.shape
    tm = ROW_TILE
    tile = lambda bb, i: (bb, i, 0)
    return pl.pallas_call(
        _outproj_kernel,
        grid=(b, s // tm),
        in_specs=[
            pl.BlockSpec((1, tm, d), tile),
            pl.BlockSpec((1, tm, NA_WIDTH), tile),
            pl.BlockSpec((1, tm, SSM_INNER), tile),
            pl.BlockSpec((1, tm, SSM_INNER), tile),
            pl.BlockSpec((1, tm, SSM_INNER), tile),
            _row_spec(d, lambda bb, i: (0, bb, 0, 2)),
            pl.BlockSpec((1, SSM_INNER), lambda bb, i: (0, 0)),
            _row_spec(d, lambda bb, i: (0, 1, 0, 0)),
            pl.BlockSpec((None,) + w.shape[1:], lambda bb, i: (0, 0, 0), pipeline_mode=pl.Buffered(1)),
        ],
        out_specs=pl.BlockSpec((1, tm, d), tile),
        out_shape=jax.ShapeDtypeStruct((b, s, d), F32),
        compiler_params=_cparams(2),
        name="outproj",
    )(x, att, yf, yb, z, mod4, ng, norm4, w)


def _ffn_kernel(x_ref, prev_ref, next_ref, sh_ref, sc_ref, gate_ref, gpre_ref, gpost_ref,
                wup_ref, cw_ref, cb_ref, wdn_ref, o_ref, h_scr, *, tn):
    i = pl.program_id(1)
    nt = pl.num_programs(1)
    tm = x_ref.shape[1]
    hidden = wdn_ref.shape[0]
    x = x_ref[0]

    def nm(xt):
        return _norm_mod(xt, gpre_ref[...], sc_ref[...], sh_ref[...]).astype(BF16)

    zero = jnp.zeros((), BF16)
    h_scr[0:HALO, :] = jnp.where(i > 0, nm(prev_ref[0]), zero)
    h_scr[HALO:HALO + tm, :] = nm(x)
    h_scr[HALO + tm:, :] = jnp.where(i < nt - 1, nm(next_ref[0]), zero)
    gated = []
    for j in range(hidden // tn):
        u = _dot(h_scr[...], wup_ref[:, j * tn:(j + 1) * tn])
        v = _dot(h_scr[HALO:HALO + tm, :], wup_ref[:, hidden + j * tn:hidden + (j + 1) * tn])
        cv = _conv3_rows(u, cw_ref[:, j * tn:(j + 1) * tn], tm) + cb_ref[:, j * tn:(j + 1) * tn]
        gl = 0.5 * cv * (1.0 + lax.erf(cv * 0.7071067811865476)) * v
        gated.append(gl.astype(BF16))
    acc = _dot(jnp.concatenate(gated, axis=1), wdn_ref[...])
    o_ref[0] = _gated_norm_add(x, acc, gpost_ref[...], gate_ref[...])


def _ffn_call(x, layer, mod4, norm4, wup, cw, cb, wdn):
    b, s, d = x.shape
    hidden = wdn.shape[1]
    tm = FFN_ROW_TILE
    tile = lambda bb, i: (bb, i, 0)
    at_layer = lambda bb, i: (layer, 0, 0)
    return pl.pallas_call(
        functools.partial(_ffn_kernel, tn=FFN_COL_TILE),
        grid=(b, s // tm),
        in_specs=[pl.BlockSpec((1, tm, d), tile)] + _halo_specs(tm, d, s) + [
            _row_spec(d, lambda bb, i: (layer, bb, 0, 3)),
            _row_spec(d, lambda bb, i: (layer, bb, 0, 4)),
            _row_spec(d, lambda bb, i: (layer, bb, 0, 5)),
            _row_spec(d, lambda bb, i: (layer, 2, 0, 0)),
            _row_spec(d, lambda bb, i: (layer, 3, 0, 0)),
            pl.BlockSpec((None, d, 2 * hidden), at_layer, pipeline_mode=pl.Buffered(1)),
            pl.BlockSpec((None,) + cw.shape[1:], at_layer),
            pl.BlockSpec((None, 1, hidden), at_layer),
            pl.BlockSpec((None, hidden, d), at_layer, pipeline_mode=pl.Buffered(1)),
        ],
        out_specs=pl.BlockSpec((1, tm, d), tile),
        out_shape=jax.ShapeDtypeStruct((b, s, d), F32),
        scratch_shapes=[pltpu.VMEM((tm + 2 * HALO, d), BF16)],
        compiler_params=_cparams(2),
        name="conv_ffn",
    )(x, x, x, mod4, mod4, mod4, norm4, norm4, wup, cw, cb, wdn)


def _window_sum(h_ext, w, rows):
    n = h_ext.shape[0]
    acc = h_ext
    span = 1
    while span < w:
        acc = acc + pltpu.roll(acc, span, 0)
        span *= 2
    shift = w // 2 - 1
    if shift:
        acc = pltpu.roll(acc, n - shift, 0)
    return acc[HALO:HALO + rows]


def _pool_kernel(x_ref, prev_ref, next_ref, sh_ref, sc_ref, gate_ref, gpre_ref, gpost_ref,
                 pw_ref, pb_ref, ps_ref, o_ref, h_scr, *, seq):
    i = pl.program_id(1)
    nt = pl.num_programs(1)
    ts = x_ref.shape[1]
    x = x_ref[0]
    gwidth = pw_ref.shape[1]

    def nm(xt):
        return _norm_mod(xt, gpre_ref[...], sc_ref[...], sh_ref[...])

    h_scr[0:HALO, :] = jnp.where(i > 0, nm(prev_ref[0]), 0.0)
    h_scr[HALO:HALO + ts, :] = nm(x)
    h_scr[HALO + ts:, :] = jnp.where(i < nt - 1, nm(next_ref[0]), 0.0)
    t = i * ts + lax.broadcasted_iota(jnp.int32, (ts, 1), 0)
    ys = []
    for gi, w in enumerate(POOL_WINDOWS):
        cols = slice(gi * gwidth, (gi + 1) * gwidth)
        h_ext = h_scr[:, cols]
        cnt = (jnp.minimum(t + w // 2, seq) - jnp.maximum(t - w // 2, 0)).astype(F32)
        pooled = _window_sum(h_ext, w, ts) / cnt - h_ext[HALO:HALO + ts]
        ys.append(_dot(pooled.astype(BF16), pw_ref[gi]) + pb_ref[:, cols])
    y = jnp.concatenate(ys, axis=-1) * ps_ref[...]
    o_ref[0] = _gated_norm_add(x, y, gpost_ref[...], gate_ref[...])


def _pool_call(x, layer, mod4, norm4, pw, pb, ps):
    b, s, d = x.shape
    ts = ROW_TILE
    tile = lambda bb, i: (bb, i, 0)
    fixed = lambda bb, i: (0, 0)
    return pl.pallas_call(
        functools.partial(_pool_kernel, seq=s),
        grid=(b, s // ts),
        in_specs=[pl.BlockSpec((1, ts, d), tile)] + _halo_specs(ts, d, s) + [
            _row_spec(d, lambda bb, i: (layer, bb, 0, 0)),
            _row_spec(d, lambda bb, i: (layer, bb, 0, 1)),
            _row_spec(d, lambda bb, i: (layer, bb, 0, 2)),
            _row_spec(d, lambda bb, i: (layer, 0, 0, 0)),
            _row_spec(d, lambda bb, i: (layer, 1, 0, 0)),
            pl.BlockSpec(pw.shape, lambda bb, i: (0, 0, 0)),
            pl.BlockSpec((1, d), fixed),
            pl.BlockSpec((1, d), fixed),
        ],
        out_specs=pl.BlockSpec((1, ts, d), tile),
        out_shape=jax.ShapeDtypeStruct((b, s, d), F32),
        scratch_shapes=[pltpu.VMEM((ts + 2 * HALO, d), F32)],
        compiler_params=_cparams(2),
        name="pool_mixer",
    )(x, x, x, mod4, mod4, mod4, norm4, norm4, pw, pb, ps)


def kernel(x, c, ctx, c_ctx, ada_w, ada_b, norm_g, w_in, w_out, na_rpb, ssm_conv_w, ssm_conv_b, ssm_a_log, ssm_dt_bias, ssm_d, ssm_norm_g, pool_w, pool_b, pool_scale, ffn_w_up, ffn_conv_w, ffn_conv_b, ffn_w_down):
    b, s, d = x.shape
    l = ctx.shape[1]
    depth = ada_w.shape[0]
    hidden = ffn_w_down.shape[1]
    rows = s // GRID_W
    assert depth == 2 and l % (SSD_STEP_CHUNKS * SSD_CHUNK) == 0 and s % l == 0 and s % (NA_STEP_BLOCKS * NA_QROWS * GRID_W) == 0
    assert s % FFN_ROW_TILE == 0 and s % ROW_TILE == 0 and hidden % FFN_COL_TILE == 0
    assert rows >= NA_KROWS and 2 * SSM_HEADS <= LANES
    assert NA_DOFF + 2 * NA_WIN_ROWS - 1 <= NA_NTILES + 1 and NA_KROWS + NA_QROWS + NA_DOFF <= NA_NTILES + 1

    rows_c = -(-(b + 1) // 8) * 8
    c_ext = jnp.zeros((rows_c, d), F32).at[:b].set(c).at[b].set(c_ctx)
    mod4 = _ada_call(c_ext, ada_w, ada_b).reshape(depth, rows_c, 1, 6 * d)
    norm4 = norm_g.reshape(depth, 4, 1, d)
    wup = ffn_w_up.astype(BF16)
    wdn = ffn_w_down.astype(BF16)
    ffn_cb = ffn_conv_b.reshape(depth, 1, hidden)

    w_cat = jnp.pad(w_in, ((0, 0), (0, 0), (0, LANES - 2 * SSM_HEADS))).astype(BF16)
    (q, z, k, v, xbc, dt_raw), (kc, vc, xbc_c, dt_c) = _inproj_call(
        x, ctx, mod4, norm4, w_cat, ssm_conv_w, ssm_conv_b.reshape(1, 1, SSM_CONV_DIM))
    att = _na_call(q, k, v, kc, vc, _na_bias_tiles(na_rpb[0]))
    pad = LANES - 2 * SSM_HEADS
    prm = jnp.stack([jnp.pad(ssm_dt_bias[0].reshape(-1), (0, pad)), jnp.pad(ssm_a_log[0].reshape(-1), (0, pad))])
    dsk = jnp.repeat(ssm_d[0], SSM_HEAD_DIM, axis=1)
    y_f, y_b = _ssd_call(xbc, dt_raw, xbc_c, dt_c, prm, dsk)
    x = _outproj_call(x, att, y_f, y_b, z, mod4, norm4, ssm_norm_g.reshape(1, SSM_INNER), w_out.astype(BF16))
    x = _ffn_call(x, 0, mod4, norm4, wup, ffn_conv_w, ffn_cb, wdn)

    x = _pool_call(x, 1, mod4, norm4, pool_w[0].astype(BF16), pool_b.reshape(1, d), pool_scale.reshape(1, d))
    x = _ffn_call(x, 1, mod4, norm4, wup, ffn_conv_w, ffn_cb, wdn)
    return x
```

```python
import functools

import numpy as np
import jax
import jax.numpy as jnp
from jax import lax
from jax.experimental import pallas as pl
from jax.experimental.pallas import tpu as pltpu

F32 = jnp.float32
BF16 = jnp.bfloat16

GRID_W = 64
NA_HEADS = 8
NA_HEAD_DIM = 64
NA_WIDTH = NA_HEADS * NA_HEAD_DIM
NA_WIN_ROWS = 8
NA_WIN_COLS = 16
SSM_HEADS = 16
SSM_HEAD_DIM = 64
SSM_INNER = SSM_HEADS * SSM_HEAD_DIM
SSM_GROUPS = 4
SSM_STATE = 128
SSD_CHUNK = 128
SSD_STEP_CHUNKS = 2
SSM_BC = SSM_GROUPS * SSM_STATE
SSM_CONV_DIM = SSM_INNER + 2 * SSM_BC
POOL_WINDOWS = (2, 4, 8, 16)
RMS_EPS = 1e-6

LANES = 128
HALO = 16
NA_QROWS = 8
NA_KROWS = 16
NA_STEP_BLOCKS = 4
NA_ROW_GROUPS = 2
NA_DOFF = NA_QROWS
NA_NTILES = 32
NEG = -1e30
VMEM_LIMIT = 56 * 1024 * 1024
ROW_TILE = 1024
FFN_ROW_TILE = 1024
FFN_COL_TILE = 256


def _cparams(n_axes):
    return pltpu.CompilerParams(dimension_semantics=("arbitrary",) * n_axes, vmem_limit_bytes=VMEM_LIMIT)


def _rms_scale(x):
    return lax.rsqrt(jnp.mean(x * x, axis=-1, keepdims=True) + RMS_EPS)


def _norm_mod(x, g, sc, sh):
    return x * _rms_scale(x) * (g * (1.0 + sc)) + sh


def _gated_norm_add(x, y, g, gate):
    return x + y * _rms_scale(y) * (g * gate)


def _silu(x):
    return x * jax.nn.sigmoid(x)


def _dot(a, b):
    return jnp.dot(a, b, preferred_element_type=F32)


def _dot_nt(a, b):
    return lax.dot_general(a, b, (((1,), (1,)), ((), ())), preferred_element_type=F32)


def _split3(a):
    hi = a.astype(BF16)
    r = a - hi.astype(F32)
    mid = r.astype(BF16)
    lo = (r - mid.astype(F32)).astype(BF16)
    return hi, mid, lo


def _conv3_rows(u, w_rows, rows, seq=None):
    n = u.shape[0]
    u_prev = pltpu.roll(u, 1, 0)[HALO:HALO + rows]
    u_next = pltpu.roll(u, n - 1, 0)[HALO:HALO + rows]
    if seq is not None:
        pos = lax.rem(lax.broadcasted_iota(jnp.int32, (rows, 1), 0), seq)
        u_prev = jnp.where(pos == 0, 0.0, u_prev)
        u_next = jnp.where(pos == seq - 1, 0.0, u_next)
    return u_prev * w_rows[0:1] + u[HALO:HALO + rows] * w_rows[1:2] + u_next * w_rows[2:3]


def _row_spec(d, index):
    return pl.BlockSpec((None, None, 1, d), index)


def _halo_specs(tm, d, s):
    rh = tm // HALO
    return [
        pl.BlockSpec((1, HALO, d), lambda bb, i: (bb, jnp.maximum(i * rh - 1, 0), 0)),
        pl.BlockSpec((1, HALO, d), lambda bb, i: (bb, jnp.minimum((i + 1) * rh, s // HALO - 1), 0)),
    ]


def _ada_kernel(c_ref, w_ref, b_ref, o_ref):
    s = _silu(c_ref[...])
    w = w_ref[0]
    acc = jnp.zeros(o_ref.shape[1:], F32)
    s_parts = _split3(s)[:2]
    w_parts = _split3(w)[:2]
    for si, sp in enumerate(s_parts):
        for wi, wp in enumerate(w_parts):
            if si + wi <= 1:
                acc = acc + _dot(sp, wp)
    o_ref[0] = acc + b_ref[0]


def _ada_call(c_ext, ada_w, ada_b):
    depth, d, n = ada_w.shape
    r = c_ext.shape[0]
    tn = 1536
    return pl.pallas_call(
        _ada_kernel,
        grid=(depth, n // tn),
        in_specs=[
            pl.BlockSpec((r, d), lambda l, j: (0, 0)),
            pl.BlockSpec((1, d, tn), lambda l, j: (l, 0, j)),
            pl.BlockSpec((1, 1, tn), lambda l, j: (l, 0, j)),
        ],
        out_specs=pl.BlockSpec((1, r, tn), lambda l, j: (l, 0, j)),
        out_shape=jax.ShapeDtypeStruct((depth, r, n), F32),
        compiler_params=_cparams(2),
        name="ada_mod",
    )(c_ext, ada_w, ada_b.reshape(depth, 1, n))


def _inproj_body(h_scr, w_ref, cw_ref, cb_ref, qz_refs, k_ref, v_ref, xbc_ref, dt_ref, tm, seq=None):
    hb = h_scr[HALO:HALO + tm, :]
    if qz_refs is not None:
        q_ref, z_ref = qz_refs
        q_ref[0] = _dot(hb, w_ref[:, 0:NA_WIDTH]).astype(q_ref.dtype)
        z_ref[0] = _dot(hb, w_ref[:, NA_WIDTH:NA_WIDTH + SSM_INNER]).astype(z_ref.dtype)
    o = NA_WIDTH + SSM_INNER
    k_ref[0] = _dot(hb, w_ref[:, o:o + NA_WIDTH]).astype(k_ref.dtype)
    o += NA_WIDTH
    v_ref[0] = _dot(hb, w_ref[:, o:o + NA_WIDTH]).astype(v_ref.dtype)
    o += NA_WIDTH
    cblk = 512
    for c0 in range(0, SSM_CONV_DIM, cblk):
        u = _dot(h_scr[...], w_ref[:, o + c0:o + c0 + cblk])
        y = _conv3_rows(u, cw_ref[:, c0:c0 + cblk], tm, seq) + cb_ref[:, c0:c0 + cblk]
        xbc_ref[0, :, c0:c0 + cblk] = _silu(y).astype(xbc_ref.dtype)
    o += SSM_CONV_DIM
    dt_ref[0] = _dot(hb, w_ref[:, o:o + LANES]).astype(dt_ref.dtype)


def _inproj_lat_kernel(x_ref, prev_ref, next_ref, sh_ref, sc_ref, g_ref, w_ref, cw_ref, cb_ref,
                       q_ref, z_ref, k_ref, v_ref, xbc_ref, dt_ref, h_scr):
    i = pl.program_id(1)
    nt = pl.num_programs(1)
    tm = x_ref.shape[1]

    def nm(xt):
        return _norm_mod(xt, g_ref[...], sc_ref[...], sh_ref[...]).astype(BF16)

    zero = jnp.zeros((), BF16)
    h_scr[0:HALO, :] = jnp.where(i > 0, nm(prev_ref[0]), zero)
    h_scr[HALO:HALO + tm, :] = nm(x_ref[0])
    h_scr[HALO + tm:, :] = jnp.where(i < nt - 1, nm(next_ref[0]), zero)
    _inproj_body(h_scr, w_ref, cw_ref, cb_ref, (q_ref, z_ref), k_ref, v_ref, xbc_ref, dt_ref, tm)


def _inproj_ctx_kernel(x_ref, sh_ref, sc_ref, g_ref, w_ref, cw_ref, cb_ref, k_ref, v_ref, xbc_ref, dt_ref, h_scr,
                       *, seq):
    tm = x_ref.shape[1]
    h_scr[0:HALO, :] = jnp.zeros((HALO, h_scr.shape[1]), BF16)
    h_scr[HALO:HALO + tm, :] = _norm_mod(x_ref[0], g_ref[...], sc_ref[...], sh_ref[...]).astype(BF16)
    h_scr[HALO + tm:, :] = jnp.zeros((HALO, h_scr.shape[1]), BF16)
    _inproj_body(h_scr, w_ref, cw_ref, cb_ref, None, k_ref, v_ref, xbc_ref, dt_ref, tm, seq)


def _inproj_call(x, ctx, mod4, norm4, w, cw, cb):
    b, s, d = x.shape
    l = ctx.shape[1]
    tm = ROW_TILE
    n = w.shape[-1]
    tile = lambda bb, i: (bb, i, 0)
    fixed = lambda bb, i: (0, 0, 0)
    weights = [
        pl.BlockSpec((None, d, n), fixed, pipeline_mode=pl.Buffered(1)),
        pl.BlockSpec((None,) + cw.shape[1:], fixed),
        pl.BlockSpec((None,) + cb.shape[1:], fixed),
    ]
    widths = (NA_WIDTH, NA_WIDTH, SSM_CONV_DIM, LANES)
    dtypes = (BF16, BF16, BF16, F32)
    q, z, k, v, xbc, dt = pl.pallas_call(
        _inproj_lat_kernel,
        grid=(b, s // tm),
        in_specs=[pl.BlockSpec((1, tm, d), tile)] + _halo_specs(tm, d, s) + [
            _row_spec(d, lambda bb, i: (0, bb, 0, 0)),
            _row_spec(d, lambda bb, i: (0, bb, 0, 1)),
            _row_spec(d, lambda bb, i: (0, 0, 0, 0)),
        ] + weights,
        out_specs=[pl.BlockSpec((1, tm, NA_WIDTH), tile), pl.BlockSpec((1, tm, SSM_INNER), tile)]
        + [pl.BlockSpec((1, tm, wd), tile) for wd in widths],
        out_shape=[jax.ShapeDtypeStruct((b, s, NA_WIDTH), BF16), jax.ShapeDtypeStruct((b, s, SSM_INNER), BF16)]
        + [jax.ShapeDtypeStruct((b, s, wd), dt_) for wd, dt_ in zip(widths, dtypes)],
        scratch_shapes=[pltpu.VMEM((tm + 2 * HALO, d), BF16)],
        compiler_params=_cparams(2),
        name="inproj",
    )(x, x, x, mod4, mod4, norm4, w, cw, cb)

    rows_c = b * l
    tc = tm if (rows_c % tm == 0 and tm % l == 0) else l
    ctx_out = pl.pallas_call(
        functools.partial(_inproj_ctx_kernel, seq=l),
        grid=(1, rows_c // tc),
        in_specs=[
            pl.BlockSpec((1, tc, d), tile),
            _row_spec(d, lambda bb, i: (0, b, 0, 0)),
            _row_spec(d, lambda bb, i: (0, b, 0, 1)),
            _row_spec(d, lambda bb, i: (0, 0, 0, 0)),
        ] + weights,
        out_specs=[pl.BlockSpec((1, tc, wd), tile) for wd in widths],
        out_shape=[jax.ShapeDtypeStruct((1, rows_c, wd), dt_) for wd, dt_ in zip(widths, dtypes)],
        scratch_shapes=[pltpu.VMEM((tc + 2 * HALO, d), BF16)],
        compiler_params=_cparams(2),
        name="inproj_ctx",
    )(ctx.reshape(1, rows_c, d), mod4, mod4, norm4, w, cw, cb)
    kc, vc, xbc_c, dt_c = [a.reshape(b, l, a.shape[-1]) for a in ctx_out]
    return (q, z, k, v, xbc, dt), (kc, vc, xbc_c, dt_c)


def _na_bias_tiles(rpb):
    h, n_rel_r, n_rel_c = rpb.shape
    w = GRID_W
    c = np.arange(w)[:, None]
    kc = np.arange(w)[None, :]
    cs = np.clip(c - NA_WIN_COLS // 2, 0, w - NA_WIN_COLS)
    col_ok = (kc >= cs) & (kc < cs + NA_WIN_COLS)
    pick = ((kc - c + NA_WIN_COLS - 1)[None] == np.arange(n_rel_c)[:, None, None]) & col_ok[None]
    t = jnp.einsum("hrd,dck->hrck", rpb, jnp.asarray(pick, F32), precision=lax.Precision.HIGHEST)
    t = jnp.where(jnp.asarray(col_ok)[None, None], t, NEG)
    n_after = NA_NTILES + 1 - NA_DOFF - n_rel_r
    tz = jnp.concatenate(
        [jnp.full((h, NA_DOFF, w, w), NEG, F32), t, jnp.full((h, n_after, w, w), NEG, F32)], axis=1)
    return jnp.concatenate([tz[:, :-1], tz[:, 1:]], axis=-1)


def _na_window_rows(kind, rows):
    r0 = {"first": 0, "interior": NA_QROWS, "last": rows - NA_QROWS}[kind]
    k0 = min(max(r0 - NA_WIN_ROWS // 2, 0), rows - NA_KROWS)
    los = [min(max(r0 + i - NA_WIN_ROWS // 2, 0), rows - NA_WIN_ROWS) - k0 for i in range(NA_QROWS)]
    return k0 - r0, los


def _na_block(q_ref, k_ref, v_ref, kc_ref, vc_ref, pt_ref, o_ref, sub, kind, rows):
    nk = NA_KROWS * GRID_W
    nq = NA_QROWS * GRID_W
    w = GRID_W
    delta, los = _na_window_rows(kind, rows)
    r0 = (pl.program_id(2) * NA_STEP_BLOCKS + sub) * NA_QROWS
    kstart = pl.multiple_of((r0 + delta) * GRID_W, GRID_W)
    q = q_ref[0, sub * nq:(sub + 1) * nq, :]
    kw = k_ref[0, pl.ds(kstart, nk), :]
    vw = v_ref[0, pl.ds(kstart, nk), :]
    kc = kc_ref[0]
    vc = vc_ref[0]
    lane = lax.broadcasted_iota(jnp.int32, (1, LANES), 1)
    one = jnp.ones((), v_ref.dtype)
    scale = jnp.asarray(NA_HEAD_DIM ** -0.5, q_ref.dtype)
    pairs = [(lo // 2, (lo + NA_WIN_ROWS - 1) // 2) for lo in los]
    gsz = NA_QROWS // NA_ROW_GROUPS
    out = jnp.zeros((nq, LANES), F32)
    for hh in range(LANES // NA_HEAD_DIM):
        sel = jnp.logical_and(lane >= hh * NA_HEAD_DIM, lane < (hh + 1) * NA_HEAD_DIM)
        qm = jnp.where(sel, q, jnp.zeros_like(q)) * scale
        vw_h = jnp.where(sel, vw, one)
        vc_h = jnp.where(sel, vc, one)
        o_groups = []
        for grp in range(NA_ROW_GROUPS):
            irange = range(grp * gsz, (grp + 1) * gsz)
            p_lo = min(pairs[i][0] for i in irange)
            p_hi = max(pairs[i][1] for i in irange)
            qg = qm[grp * gsz * w:(grp + 1) * gsz * w, :]
            s_raw = _dot_nt(qg, kw[p_lo * 2 * w:(p_hi + 1) * 2 * w, :])
            sc = _dot_nt(qg, kc)
            p_rows, pc_rows = [], []
            for i in irange:
                lo = los[i]
                jj_lo, jj_hi = pairs[i]
                ri = i - grp * gsz
                blocks = []
                for jj in range(jj_lo, jj_hi + 1):
                    t_idx = delta + 2 * jj - i + (NA_WIN_ROWS - 1) + NA_DOFF
                    blk = s_raw[ri * w:(ri + 1) * w, (jj - p_lo) * 2 * w:(jj - p_lo + 1) * 2 * w] + pt_ref[hh, t_idx]
                    left_out, right_out = 2 * jj < lo, 2 * jj + 1 >= lo + NA_WIN_ROWS
                    if left_out or right_out:
                        blk = blk + jnp.where(lane < w, NEG if left_out else 0.0, NEG if right_out else 0.0)
                    blocks.append(blk)
                s_i = jnp.concatenate(blocks, axis=1)
                sc_i = sc[ri * w:(ri + 1) * w, :]
                m = jnp.maximum(jnp.max(s_i, axis=-1, keepdims=True), jnp.max(sc_i, axis=-1, keepdims=True))
                pieces = [jnp.exp(s_i - m).astype(BF16)]
                if jj_lo > p_lo:
                    pieces.insert(0, jnp.zeros((w, (jj_lo - p_lo) * 2 * w), BF16))
                if jj_hi < p_hi:
                    pieces.append(jnp.zeros((w, (p_hi - jj_hi) * 2 * w), BF16))
                p_rows.append(jnp.concatenate(pieces, axis=1) if len(pieces) > 1 else pieces[0])
                pc_rows.append(jnp.exp(sc_i - m).astype(BF16))
            p = jnp.concatenate(p_rows, axis=0)
            pc = jnp.concatenate(pc_rows, axis=0)
            v_all = jnp.concatenate([vw_h[p_lo * 2 * w:(p_hi + 1) * 2 * w, :], vc_h], axis=0)
            o_groups.append(_dot(jnp.concatenate([p, pc], axis=1), v_all))
        o = jnp.concatenate(o_groups, axis=0)
        den = pltpu.roll(o, NA_HEAD_DIM, 1)
        out = jnp.where(sel, o / den, out)
    o_ref[0, sub * nq:(sub + 1) * nq, :] = out.astype(o_ref.dtype)


def _na_kernel(q_ref, k_ref, v_ref, kc_ref, vc_ref, pt_ref, o_ref, *, rows):
    step = pl.program_id(2)
    nsteps = rows // (NA_QROWS * NA_STEP_BLOCKS)

    def run(kinds):
        for sub, kind in enumerate(kinds):
            _na_block(q_ref, k_ref, v_ref, kc_ref, vc_ref, pt_ref, o_ref, sub, kind, rows)

    inner = ("interior",) * (NA_STEP_BLOCKS - 1)
    if nsteps == 1:
        run(("first",) + ("interior",) * (NA_STEP_BLOCKS - 2) + ("last",))
    else:
        pl.when(step == 0)(lambda: run(("first",) + inner))
        pl.when(jnp.logical_and(step > 0, step < nsteps - 1))(lambda: run(inner + ("interior",)))
        pl.when(step == nsteps - 1)(lambda: run(inner + ("last",)))


def _na_call(q, k, v, kc, vc, pt):
    b, s, _ = q.shape
    l = kc.shape[1]
    rows = s // GRID_W
    nq = NA_STEP_BLOCKS * NA_QROWS * GRID_W
    hp = LANES // NA_HEAD_DIM
    per_batch = lambda p, bb, rb: (bb, 0, p)
    return pl.pallas_call(
        functools.partial(_na_kernel, rows=rows),
        grid=(NA_WIDTH // LANES, b, s // nq),
        in_specs=[
            pl.BlockSpec((1, nq, LANES), lambda p, bb, rb: (bb, rb, p)),
            pl.BlockSpec((1, s, LANES), per_batch),
            pl.BlockSpec((1, s, LANES), per_batch),
            pl.BlockSpec((1, l, LANES), per_batch),
            pl.BlockSpec((1, l, LANES), per_batch),
            pl.BlockSpec((hp,) + pt.shape[1:], lambda p, bb, rb: (p, 0, 0, 0)),
        ],
        out_specs=pl.BlockSpec((1, nq, LANES), lambda p, bb, rb: (bb, rb, p)),
        out_shape=jax.ShapeDtypeStruct((b, s, NA_WIDTH), BF16),
        compiler_params=_cparams(3),
        name="nbr_attention",
    )(q, k, v, kc, vc, pt)


def _ssd_decays(dt_ref, prm_ref, row0, *, reverse):
    q = SSD_CHUNK
    dt = jax.nn.softplus(dt_ref[0, row0:row0 + q, :] + prm_ref[0:1, :])
    a = -dt * jnp.exp(prm_ref[1:2, :])
    ri = lax.broadcasted_iota(jnp.int32, (q, q), 0)
    ci = lax.broadcasted_iota(jnp.int32, (q, q), 1)
    tri = (ci >= ri) if reverse else (ci <= ri)
    ones = jnp.where(tri, 1.0, 0.0).astype(BF16)
    cs = jnp.zeros((q, LANES), F32)
    for part in _split3(a):
        cs = cs + _dot(ones, part)
    tot_row = cs[q - 1:q, :] if not reverse else cs[0:1, :]
    cs_t = cs.T
    dt_t = dt.T
    e_end_t = (jnp.exp(tot_row.T - cs_t) * dt_t).astype(BF16)
    seg_row = cs_t - jnp.log(dt_t)
    e_start = jnp.exp(cs)
    dec = jnp.exp(tot_row)
    return tri, cs, seg_row, e_end_t, e_start, dec


def _ssd_direction(xbc_ref, row0, decays, dsk_ref, state_ref, y_ref, *, reverse, emit_y):
    q = SSD_CHUNK
    tri, cs, seg_row, e_end_t, e_start, dec = decays
    lane_off = SSM_HEADS if reverse else 0
    lane = lax.broadcasted_iota(jnp.int32, (1, LANES), 1)
    lo = lane < SSM_HEAD_DIM
    rows = slice(row0, row0 + q)
    for g in range(SSM_GROUPS):
        bm = xbc_ref[0, rows, SSM_INNER + g * SSM_STATE:SSM_INNER + (g + 1) * SSM_STATE]
        cm = xbc_ref[0, rows, SSM_INNER + SSM_BC + g * SSM_STATE:SSM_INNER + SSM_BC + (g + 1) * SSM_STATE]
        bt = bm.astype(F32).T.astype(BF16)
        if emit_y:
            cb = _dot_nt(cm, bm).astype(BF16)
            cmf = cm.astype(F32)
        heads_per_group = SSM_HEADS // SSM_GROUPS
        for pp in range(heads_per_group // 2):
            pair = g * (heads_per_group // 2) + pp
            xs = xbc_ref[0, rows, pair * LANES:(pair + 1) * LANES]
            xs_lo = jnp.where(lo, xs, jnp.zeros_like(xs))
            xs_hi = jnp.where(lo, jnp.zeros_like(xs), xs)
            rhs_x = jnp.concatenate([xs_lo, xs_hi], axis=0)
            st = state_ref[pair]
            h0 = lane_off + 2 * pair
            btw = [bt * e_end_t[h0 + u:h0 + u + 1, :] for u in range(2)]
            upd = _dot(jnp.concatenate(btw, axis=1), rhs_x)
            if emit_y:
                stb = st.astype(BF16)
                st_lo = jnp.where(lo, stb, jnp.zeros_like(stb))
                st_hi = jnp.where(lo, jnp.zeros_like(stb), stb)
                lhs = []
                for u in range(2):
                    hcol = cs[:, h0 + u:h0 + u + 1]
                    seg = jnp.exp(jnp.where(tri, hcol - seg_row[h0 + u:h0 + u + 1, :], NEG))
                    lhs.append(cb * seg.astype(BF16))
                for u in range(2):
                    lhs.append(cm * e_start[:, h0 + u:h0 + u + 1].astype(BF16))
                y = _dot(jnp.concatenate(lhs, axis=1), jnp.concatenate([rhs_x, st_lo, st_hi], axis=0))
                y = y + dsk_ref[:, pair * LANES:(pair + 1) * LANES] * xs.astype(F32)
                y_ref[0, rows, pair * LANES:(pair + 1) * LANES] = y.astype(y_ref.dtype)
            dpair = jnp.where(lo, dec[:, h0:h0 + 1], dec[:, h0 + 1:h0 + 2])
            state_ref[pair] = st * dpair + upd


def _ssd_kernel(xf_ref, dtf_ref, xb_ref, dtb_ref, xc_ref, dtc_ref, prm_ref, dsk_ref, yf_ref, yb_ref,
                sf_ref, sb_ref, *, n_ctx):
    s = pl.program_id(1)
    step = SSD_STEP_CHUNKS * SSD_CHUNK

    @pl.when(s == 0)
    def _():
        sf_ref[...] = jnp.zeros_like(sf_ref)
        sb_ref[...] = jnp.zeros_like(sb_ref)

    def run(x_f, dt_f, x_b, dt_b, base_f, base_b, emit_y):
        order_f = [c * SSD_CHUNK for c in range(SSD_STEP_CHUNKS)]
        order_b = order_f[::-1]
        dec_f = [_ssd_decays(dt_f, prm_ref, base_f + r, reverse=False) for r in order_f]
        dec_b = [_ssd_decays(dt_b, prm_ref, base_b + r, reverse=True) for r in order_b]
        for c in range(SSD_STEP_CHUNKS):
            _ssd_direction(x_f, base_f + order_f[c], dec_f[c], dsk_ref.at[0:1], sf_ref, yf_ref,
                           reverse=False, emit_y=emit_y)
            _ssd_direction(x_b, base_b + order_b[c], dec_b[c], dsk_ref.at[1:2], sb_ref, yb_ref,
                           reverse=True, emit_y=emit_y)

    for j in range(n_ctx):
        pl.when(s == j)(functools.partial(run, xc_ref, dtc_ref, xc_ref, dtc_ref,
                                          j * step, (n_ctx - 1 - j) * step, False))

    @pl.when(s >= n_ctx)
    def _():
        run(xf_ref, dtf_ref, xb_ref, dtb_ref, 0, 0, True)


def _ssd_call(xbc, dt, xbc_c, dt_c, prm, dsk):
    b, n_lat, c = xbc.shape
    l = xbc_c.shape[1]
    q = SSD_STEP_CHUNKS * SSD_CHUNK
    nl = n_lat // q
    n_ctx = l // q
    fwd = lambda bb, s: (bb, jnp.maximum(s - n_ctx, 0), 0)
    bwd = lambda bb, s: (bb, jnp.minimum(nl - 1 - (s - n_ctx), nl - 1), 0)
    whole = lambda bb, s: (bb, 0, 0)
    fixed = lambda bb, s: (0, 0)
    return pl.pallas_call(
        functools.partial(_ssd_kernel, n_ctx=n_ctx),
        grid=(b, n_ctx + nl),
        in_specs=[
            pl.BlockSpec((1, q, c), fwd),
            pl.BlockSpec((1, q, LANES), fwd),
            pl.BlockSpec((1, q, c), bwd),
            pl.BlockSpec((1, q, LANES), bwd),
            pl.BlockSpec((1, l, c), whole),
            pl.BlockSpec((1, l, LANES), whole),
            pl.BlockSpec((2, LANES), fixed),
            pl.BlockSpec((2, SSM_INNER), fixed),
        ],
        out_specs=[
            pl.BlockSpec((1, q, SSM_INNER), fwd),
            pl.BlockSpec((1, q, SSM_INNER), bwd),
        ],
        out_shape=[jax.ShapeDtypeStruct((b, n_lat, SSM_INNER), BF16)] * 2,
        scratch_shapes=[pltpu.VMEM((SSM_HEADS // 2, SSM_STATE, LANES), F32)] * 2,
        compiler_params=_cparams(2),
        name="ssd_scan",
    )(xbc, dt, xbc, dt, xbc_c, dt_c, prm, dsk)


def _outproj_kernel(x_ref, att_ref, yf_ref, yb_ref, z_ref, gate_ref, ng_ref, gpost_ref, w_ref, o_ref):
    y = yf_ref[0].astype(F32) + yb_ref[0].astype(F32)
    yz = y * _silu(z_ref[0].astype(F32))
    gw = SSM_INNER // SSM_GROUPS
    parts = []
    for g in range(SSM_GROUPS):
        seg = yz[:, g * gw:(g + 1) * gw]
        parts.append(seg * _rms_scale(seg))
    ssm = (jnp.concatenate(parts, axis=-1) * ng_ref[...]).astype(BF16)
    o = _dot(jnp.concatenate([att_ref[0], ssm], axis=-1), w_ref[...])
    o_ref[0] = _gated_norm_add(x_ref[0], o, gpost_ref[...], gate_ref[...])


def _outproj_call(x, att, yf, yb, z, mod4, norm4, ng, w):
    b, s, d = x.shape
    tm = ROW_TILE
    tile = lambda bb, i: (bb, i, 0)
    return pl.pallas_call(
        _outproj_kernel,
        grid=(b, s // tm),
        in_specs=[
            pl.BlockSpec((1, tm, d), tile),
            pl.BlockSpec((1, tm, NA_WIDTH), tile),
            pl.BlockSpec((1, tm, SSM_INNER), tile),
            pl.BlockSpec((1, tm, SSM_INNER), tile),
            pl.BlockSpec((1, tm, SSM_INNER), tile),
            _row_spec(d, lambda bb, i: (0, bb, 0, 2)),
            pl.BlockSpec((1, SSM_INNER), lambda bb, i: (0, 0)),
            _row_spec(d, lambda bb, i: (0, 1, 0, 0)),
            pl.BlockSpec((None,) + w.shape[1:], lambda bb, i: (0, 0, 0), pipeline_mode=pl.Buffered(1)),
        ],
        out_specs=pl.BlockSpec((1, tm, d), tile),
        out_shape=jax.ShapeDtypeStruct((b, s, d), F32),
        compiler_params=_cparams(2),
        name="outproj",
    )(x, att, yf, yb, z, mod4, ng, norm4, w)


def _ffn_kernel(x_ref, prev_ref, next_ref, sh_ref, sc_ref, gate_ref, gpre_ref, gpost_ref,
                wup_ref, cw_ref, cb_ref, wdn_ref, o_ref, h_scr, *, tn):
    i = pl.program_id(1)
    nt = pl.num_programs(1)
    tm = x_ref.shape[1]
    hidden = wdn_ref.shape[0]
    x = x_ref[0]

    def nm(xt):
        return _norm_mod(xt, gpre_ref[...], sc_ref[...], sh_ref[...]).astype(BF16)

    zero = jnp.zeros((), BF16)
    h_scr[0:HALO, :] = jnp.where(i > 0, nm(prev_ref[0]), zero)
    h_scr[HALO:HALO + tm, :] = nm(x)
    h_scr[HALO + tm:, :] = jnp.where(i < nt - 1, nm(next_ref[0]), zero)
    gated = []
    for j in range(hidden // tn):
        u = _dot(h_scr[...], wup_ref[:, j * tn:(j + 1) * tn])
        v = _dot(h_scr[HALO:HALO + tm, :], wup_ref[:, hidden + j * tn:hidden + (j + 1) * tn])
        cv = _conv3_rows(u, cw_ref[:, j * tn:(j + 1) * tn], tm) + cb_ref[:, j * tn:(j + 1) * tn]
        gl = 0.5 * cv * (1.0 + lax.erf(cv * 0.7071067811865476)) * v
        gated.append(gl.astype(BF16))
    acc = _dot(jnp.concatenate(gated, axis=1), wdn_ref[...])
    o_ref[0] = _gated_norm_add(x, acc, gpost_ref[...], gate_ref[...])


def _ffn_call(x, layer, mod4, norm4, wup, cw, cb, wdn):
    b, s, d = x.shape
    hidden = wdn.shape[1]
    tm = FFN_ROW_TILE
    tile = lambda bb, i: (bb, i, 0)
    at_layer = lambda bb, i: (layer, 0, 0)
    return pl.pallas_call(
        functools.partial(_ffn_kernel, tn=FFN_COL_TILE),
        grid=(b, s // tm),
        in_specs=[pl.BlockSpec((1, tm, d), tile)] + _halo_specs(tm, d, s) + [
            _row_spec(d, lambda bb, i: (layer, bb, 0, 3)),
            _row_spec(d, lambda bb, i: (layer, bb, 0, 4)),
            _row_spec(d, lambda bb, i: (layer, bb, 0, 5)),
            _row_spec(d, lambda bb, i: (layer, 2, 0, 0)),
            _row_spec(d, lambda bb, i: (layer, 3, 0, 0)),
            pl.BlockSpec((None, d, 2 * hidden), at_layer, pipeline_mode=pl.Buffered(1)),
            pl.BlockSpec((None,) + cw.shape[1:], at_layer),
            pl.BlockSpec((None, 1, hidden), at_layer),
            pl.BlockSpec((None, hidden, d), at_layer, pipeline_mode=pl.Buffered(1)),
        ],
        out_specs=pl.BlockSpec((1, tm, d), tile),
        out_shape=jax.ShapeDtypeStruct((b, s, d), F32),
        scratch_shapes=[pltpu.VMEM((tm + 2 * HALO, d), BF16)],
        compiler_params=_cparams(2),
        name="conv_ffn",
    )(x, x, x, mod4, mod4, mod4, norm4, norm4, wup, cw, cb, wdn)


def _window_sum(h_ext, w, rows):
    n = h_ext.shape[0]
    acc = h_ext
    span = 1
    while span < w:
        acc = acc + pltpu.roll(acc, span, 0)
        span *= 2
    shift = w // 2 - 1
    if shift:
        acc = pltpu.roll(acc, n - shift, 0)
    return acc[HALO:HALO + rows]


def _pool_kernel(x_ref, prev_ref, next_ref, sh_ref, sc_ref, gate_ref, gpre_ref, gpost_ref,
                 pw_ref, pb_ref, ps_ref, o_ref, h_scr, *, seq):
    i = pl.program_id(1)
    nt = pl.num_programs(1)
    ts = x_ref.shape[1]
    x = x_ref[0]
    gwidth = pw_ref.shape[1]

    def nm(xt):
        return _norm_mod(xt, gpre_ref[...], sc_ref[...], sh_ref[...])

    h_scr[0:HALO, :] = jnp.where(i > 0, nm(prev_ref[0]), 0.0)
    h_scr[HALO:HALO + ts, :] = nm(x)
    h_scr[HALO + ts:, :] = jnp.where(i < nt - 1, nm(next_ref[0]), 0.0)
    t = i * ts + lax.broadcasted_iota(jnp.int32, (ts, 1), 0)
    ys = []
    for gi, w in enumerate(POOL_WINDOWS):
        cols = slice(gi * gwidth, (gi + 1) * gwidth)
        h_ext = h_scr[:, cols]
        cnt = (jnp.minimum(t + w // 2, seq) - jnp.maximum(t - w // 2, 0)).astype(F32)
        pooled = _window_sum(h_ext, w, ts) / cnt - h_ext[HALO:HALO + ts]
        ys.append(_dot(pooled.astype(BF16), pw_ref[gi]) + pb_ref[:, cols])
    y = jnp.concatenate(ys, axis=-1) * ps_ref[...]
    o_ref[0] = _gated_norm_add(x, y, gpost_ref[...], gate_ref[...])


def _pool_call(x, layer, mod4, norm4, pw, pb, ps):
    b, s, d = x.shape
    ts = ROW_TILE
    tile = lambda bb, i: (bb, i, 0)
    fixed = lambda bb, i: (0, 0)
    return pl.pallas_call(
        functools.partial(_pool_kernel, seq=s),
        grid=(b, s // ts),
        in_specs=[pl.BlockSpec((1, ts, d), tile)] + _halo_specs(ts, d, s) + [
            _row_spec(d, lambda bb, i: (layer, bb, 0, 0)),
            _row_spec(d, lambda bb, i: (layer, bb, 0, 1)),
            _row_spec(d, lambda bb, i: (layer, bb, 0, 2)),
            _row_spec(d, lambda bb, i: (layer, 0, 0, 0)),
            _row_spec(d, lambda bb, i: (layer, 1, 0, 0)),
            pl.BlockSpec(pw.shape, lambda bb, i: (0, 0, 0)),
            pl.BlockSpec((1, d), fixed),
            pl.BlockSpec((1, d), fixed),
        ],
        out_specs=pl.BlockSpec((1, ts, d), tile),
        out_shape=jax.ShapeDtypeStruct((b, s, d), F32),
        scratch_shapes=[pltpu.VMEM((ts + 2 * HALO, d), F32)],
        compiler_params=_cparams(2),
        name="pool_mixer",
    )(x, x, x, mod4, mod4, mod4, norm4, norm4, pw, pb, ps)


def kernel(x, c, ctx, c_ctx, ada_w, ada_b, norm_g, w_in, w_out, na_rpb, ssm_conv_w, ssm_conv_b, ssm_a_log, ssm_dt_bias, ssm_d, ssm_norm_g, pool_w, pool_b, pool_scale, ffn_w_up, ffn_conv_w, ffn_conv_b, ffn_w_down):
    b, s, d = x.shape
    l = ctx.shape[1]
    depth = ada_w.shape[0]
    hidden = ffn_w_down.shape[1]
    rows = s // GRID_W
    assert depth == 2 and l % (SSD_STEP_CHUNKS * SSD_CHUNK) == 0 and s % l == 0 and s % (NA_STEP_BLOCKS * NA_QROWS * GRID_W) == 0
    assert s % FFN_ROW_TILE == 0 and s % ROW_TILE == 0 and hidden % FFN_COL_TILE == 0
    assert rows >= NA_KROWS and 2 * SSM_HEADS <= LANES
    assert NA_DOFF + 2 * NA_WIN_ROWS - 1 <= NA_NTILES + 1 and NA_KROWS + NA_QROWS + NA_DOFF <= NA_NTILES + 1

    rows_c = -(-(b + 1) // 8) * 8
    c_ext = jnp.zeros((rows_c, d), F32).at[:b].set(c).at[b].set(c_ctx)
    mod4 = _ada_call(c_ext, ada_w, ada_b).reshape(depth, rows_c, 1, 6 * d)
    norm4 = norm_g.reshape(depth, 4, 1, d)
    wup = ffn_w_up.astype(BF16)
    wdn = ffn_w_down.astype(BF16)
    ffn_cb = ffn_conv_b.reshape(depth, 1, hidden)

    w_cat = jnp.pad(w_in, ((0, 0), (0, 0), (0, LANES - 2 * SSM_HEADS))).astype(BF16)
    (q, z, k, v, xbc, dt_raw), (kc, vc, xbc_c, dt_c) = _inproj_call(
        x, ctx, mod4, norm4, w_cat, ssm_conv_w, ssm_conv_b.reshape(1, 1, SSM_CONV_DIM))
    att = _na_call(q, k, v, kc, vc, _na_bias_tiles(na_rpb[0]))
    pad = LANES - 2 * SSM_HEADS
    prm = jnp.stack([jnp.pad(ssm_dt_bias[0].reshape(-1), (0, pad)), jnp.pad(ssm_a_log[0].reshape(-1), (0, pad))])
    dsk = jnp.repeat(ssm_d[0], SSM_HEAD_DIM, axis=1)
    y_f, y_b = _ssd_call(xbc, dt_raw, xbc_c, dt_c, prm, dsk)
    x = _outproj_call(x, att, y_f, y_b, z, mod4, norm4, ssm_norm_g.reshape(1, SSM_INNER), w_out.astype(BF16))
    x = _ffn_call(x, 0, mod4, norm4, wup, ffn_conv_w, ffn_cb, wdn)

    x = _pool_call(x, 1, mod4, norm4, pool_w[0].astype(BF16), pool_b.reshape(1, d), pool_scale.reshape(1, d))
    x = _ffn_call(x, 1, mod4, norm4, wup, ffn_conv_w, ffn_cb, wdn)
    return x
```
